```python
import jax, jax.numpy as jnp
from jax import lax
import numpy as np

D_MODEL = 2048
BATCH = 1
SEQ = 8192
DEPTH = 2
DEC_BATCH = 128
DEC_SEQ = 1
PAST_LEN = 2048
PAGE_SIZE = 128

N_MIXERS = 2
N_ATTN_LAYERS = (DEPTH + 1) // 2
N_CONV_LAYERS = DEPTH // 2
N_HEADS = 16
HEAD_DIM = 128
QKV_W = N_HEADS * HEAD_DIM
IDX_HEADS = 16
IDX_DIM = 128
TOPK_MAX = 256
Q_BLOCK = 128
ATTN_SPLITS = [QKV_W, 2 * QKV_W, 3 * QKV_W, 3 * QKV_W + IDX_HEADS * IDX_DIM, 3 * QKV_W + IDX_HEADS * IDX_DIM + IDX_DIM]
ATTN_IN_COLS = ATTN_SPLITS[-1] + IDX_HEADS
CONV_WIDTH = 3
FFN_DIM = 7168
N_EXPERTS = 8
MOE_TOP_K = 2
MOE_BLOCK = 128
ROPE_THETA = 10000.0
RMS_EPS = 1e-6
NEG_INF = -1e30

kernel_name = "dsa_shortconv_moe_adaln_decode_step"


def rms_norm(x, g):
    xf = x.astype(jnp.float32)
    y = xf * lax.rsqrt(jnp.mean(xf * xf, axis=-1, keepdims=True) + RMS_EPS)
    return (y * g.astype(jnp.float32)).astype(x.dtype)


def adaln(c, w, b):
    m = (jax.nn.silu(c) @ w + b)[:, None, :]
    return jnp.split(m, 6, axis=-1)


def modulate(h, shift, scale):
    return h * (1 + scale) + shift


def rope(x, pos):
    half = x.shape[-1] // 2
    inv = ROPE_THETA ** (-jnp.arange(half, dtype=jnp.float32) / half)
    ang = pos.astype(jnp.float32)[:, None] * inv[None, :]
    cos = jnp.cos(ang)[:, None, :]
    sin = jnp.sin(ang)[:, None, :]
    xf = x.astype(jnp.float32)
    x1, x2 = xf[..., :half], xf[..., half:]
    return jnp.concatenate([x1 * cos - x2 * sin, x2 * cos + x1 * sin], axis=-1).astype(x.dtype)


def dsa_project(h, w_in, pos):
    B, T, _ = h.shape
    q, k, v, qi, ki, wi = jnp.split(h @ w_in, ATTN_SPLITS, axis=-1)
    q = rope(q.reshape(B, T, N_HEADS, HEAD_DIM), pos)
    k = rope(k.reshape(B, T, N_HEADS, HEAD_DIM), pos)
    v = v.reshape(B, T, N_HEADS, HEAD_DIM)
    qi = rope(qi.reshape(B, T, IDX_HEADS, IDX_DIM), pos)
    ki = rope(ki[:, :, None, :], pos)[:, :, 0, :]
    return q, k, v, qi, ki, wi


def index_scores(qi, wi, ki):
    dots = jnp.einsum('bqhd,bsd->bqhs', qi.astype(jnp.float32), ki.astype(jnp.float32)) * IDX_DIM ** -0.5
    return jnp.einsum('bqh,bqhs->bqs', wi.astype(jnp.float32) * IDX_HEADS ** -0.5, jax.nn.relu(dots))


def sparse_attend(q, k_sel, v_sel, valid):
    logits = jnp.einsum('bqhd,bqkhd->bqhk', q.astype(jnp.float32), k_sel.astype(jnp.float32)) * HEAD_DIM ** -0.5
    logits = jnp.where(valid[:, :, None, :], logits, NEG_INF)
    p = jax.nn.softmax(logits, axis=-1)
    return jnp.einsum('bqhk,bqkhd->bqhd', p, v_sel.astype(jnp.float32)).astype(q.dtype)


def gather_rows(rows, idx):
    return jax.vmap(lambda r, i: r[i])(rows, idx)


def dsa_prompt(h, w_in, w_out):
    B, S, _ = h.shape
    pos = jnp.arange(S)
    q, k, v, qi, ki, wi = dsa_project(h, w_in, pos)
    topk = min(TOPK_MAX, S // 4)
    n_blocks = S // Q_BLOCK

    def block(i):
        start = i * Q_BLOCK
        qpos = start + jnp.arange(Q_BLOCK)
        q_b = lax.dynamic_slice_in_dim(q, start, Q_BLOCK, axis=1)
        qi_b = lax.dynamic_slice_in_dim(qi, start, Q_BLOCK, axis=1)
        wi_b = lax.dynamic_slice_in_dim(wi, start, Q_BLOCK, axis=1)
        scores = index_scores(qi_b, wi_b, ki)
        scores = jnp.where((pos[None, :] <= qpos[:, None])[None], scores, NEG_INF)
        _, idx = lax.top_k(scores, topk)
        valid = idx <= qpos[None, :, None]
        return sparse_attend(q_b, gather_rows(k, idx), gather_rows(v, idx), valid)

    out = lax.map(block, jnp.arange(n_blocks))
    out = jnp.transpose(out, (1, 0, 2, 3, 4)).reshape(B, S, QKV_W)
    return out @ w_out, k, v, ki


def dsa_sample(h, cache_k, cache_v, cache_ki, page_table, w_in, w_out):
    DB, T, _ = h.shape
    n_pages = page_table.shape[1]
    past = n_pages * PAGE_SIZE
    pos = past + jnp.arange(T)
    q, k, v, qi, ki, wi = dsa_project(h, w_in, pos)
    ki_past = cache_ki[page_table].reshape(DB, past, IDX_DIM)
    ki_all = jnp.concatenate([ki_past, ki.astype(ki_past.dtype)], axis=1)
    L = past + T
    topk = min(TOPK_MAX, L // 4)
    scores = index_scores(qi, wi, ki_all)
    scores = jnp.where((jnp.arange(L)[None, :] <= pos[:, None])[None], scores, NEG_INF)
    _, idx = lax.top_k(scores, topk)
    in_past = idx < past
    pidx = jnp.minimum(idx, past - 1)
    phys = jnp.take_along_axis(page_table, (pidx // PAGE_SIZE).reshape(DB, -1), axis=1).reshape(pidx.shape)
    slot = pidx % PAGE_SIZE
    nidx = jnp.clip(idx - past, 0, T - 1)
    sel = in_past[..., None, None]
    k_sel = jnp.where(sel, cache_k[phys, slot], gather_rows(k, nidx).astype(cache_k.dtype))
    v_sel = jnp.where(sel, cache_v[phys, slot], gather_rows(v, nidx).astype(cache_v.dtype))
    valid = idx <= pos[None, :, None]
    out = sparse_attend(q, k_sel, v_sel, valid).reshape(DB, T, QKV_W)
    return out @ w_out, k, v, ki


def short_conv(h, conv_state, w_in, conv_w, w_out):
    T = h.shape[1]
    b_gate, c_gate, u = jnp.split(h @ w_in, 3, axis=-1)
    z = c_gate * u
    zp = jnp.concatenate([conv_state.astype(z.dtype), z], axis=1)
    y = sum(conv_w[j] * zp[:, j:j + T] for j in range(CONV_WIDTH))
    return (b_gate * y) @ w_out, zp[:, -(CONV_WIDTH - 1):]


def swiglu(x, w_gu, w_down):
    g, u = jnp.split(x @ w_gu, 2, axis=-1)
    return (jax.nn.silu(g) * u) @ w_down


def moe_ffn(h, w_router, w_gu, w_down):
    B, T, D = h.shape
    n = B * T
    xf = h.reshape(n, D)
    logits = (xf @ w_router).astype(jnp.float32)
    top_val, top_exp = lax.top_k(logits, MOE_TOP_K)
    gate = jax.nn.softmax(top_val, axis=-1)
    a = n * MOE_TOP_K
    exp_flat = top_exp.reshape(a)
    tok_flat = jnp.repeat(jnp.arange(n, dtype=jnp.int32), MOE_TOP_K)
    gate_flat = gate.reshape(a)
    order = jnp.argsort(exp_flat)
    exp_sorted = exp_flat[order]
    counts = jnp.bincount(exp_flat, length=N_EXPERTS)
    starts = jnp.cumsum(counts) - counts
    padded = (counts + MOE_BLOCK - 1) // MOE_BLOCK * MOE_BLOCK
    pad_ends = jnp.cumsum(padded)
    pad_starts = pad_ends - padded
    dest = pad_starts[exp_sorted] + (jnp.arange(a) - starts[exp_sorted])
    n_blocks = -(-a // MOE_BLOCK) + N_EXPERTS
    rows = n_blocks * MOE_BLOCK
    row_tok = jnp.zeros((rows,), jnp.int32).at[dest].set(tok_flat[order])
    row_gate = jnp.zeros((rows,), jnp.float32).at[dest].set(gate_flat[order])
    block_exp = jnp.minimum(jnp.searchsorted(pad_ends, jnp.arange(n_blocks) * MOE_BLOCK, side='right'), N_EXPERTS - 1)
    xr = xf[row_tok].reshape(n_blocks, MOE_BLOCK, D)

    def expert_block(args):
        xb, e = args
        return swiglu(xb, w_gu[e], w_down[e])

    yr = lax.map(expert_block, (xr, block_exp)).reshape(rows, D)
    out = jnp.zeros((n, D), jnp.float32).at[row_tok].add(yr.astype(jnp.float32) * row_gate[:, None])
    return out.astype(h.dtype).reshape(B, T, D)


def trunk(x, c, attn_mixer, conv_mixer, norm_g, final_norm_g, w_ada, b_ada,
          w_ffn_gu, w_ffn_down, w_router, w_exp_gu, w_exp_down):
    ks, vs, kis, convs = [], [], [], []
    for i in range(DEPTH):
        j = i // 2
        sh1, sc1, g1, sh2, sc2, g2 = adaln(c, w_ada[i], b_ada[i])
        h = modulate(rms_norm(x, norm_g[i, 0]), sh1, sc1)
        if i % N_MIXERS == 0:
            mix, k, v, ki = attn_mixer(j, h)
            ks.append(k)
            vs.append(v)
            kis.append(ki)
        else:
            mix, st = conv_mixer(j, h)
            convs.append(st)
        x = x + g1 * mix
        h = modulate(rms_norm(x, norm_g[i, 1]), sh2, sc2)
        if i % 2 == 0:
            f = swiglu(h, w_ffn_gu[j], w_ffn_down[j])
        else:
            f = moe_ffn(h, w_router[j], w_exp_gu[j], w_exp_down[j])
        x = x + g2 * f
    return rms_norm(x, final_norm_g), jnp.stack(ks), jnp.stack(vs), jnp.stack(kis), jnp.stack(convs)


def setup_inputs(seed: int = 0) -> dict:
    key = jax.random.key(seed)
    ks = jax.random.split(key, 24)
    n_pages = PAST_LEN // PAGE_SIZE
    used = DEC_BATCH * n_pages
    n_pool = used + (used + 3) // 4
    D = D_MODEL

    def nrm(k, shape, scale):
        return jax.random.normal(k, shape, jnp.float32) * scale

    perm = jax.random.permutation(ks[6], n_pool)
    return {
        'x_prompt': nrm(ks[0], (BATCH, SEQ, D), 1.0),
        'x_sample': nrm(ks[1], (DEC_BATCH, DEC_SEQ, D), 1.0),
        'cache_k': nrm(ks[2], (N_ATTN_LAYERS, n_pool, PAGE_SIZE, N_HEADS, HEAD_DIM), 1.0),
        'cache_v': nrm(ks[3], (N_ATTN_LAYERS, n_pool, PAGE_SIZE, N_HEADS, HEAD_DIM), 1.0),
        'cache_kidx': nrm(ks[4], (N_ATTN_LAYERS, n_pool, PAGE_SIZE, IDX_DIM), 1.0),
        'state_conv': nrm(ks[5], (N_CONV_LAYERS, DEC_BATCH, CONV_WIDTH - 1, D), 1.0),
        'page_table': perm[:used].reshape(DEC_BATCH, n_pages).astype(jnp.int32),
        'c_prompt': nrm(ks[7], (BATCH, D), 1.0),
        'c_sample': nrm(ks[8], (DEC_BATCH, D), 1.0),
        'norm_g': 1.0 + nrm(ks[9], (DEPTH, 2, D), 0.1),
        'final_norm_g': 1.0 + nrm(ks[10], (D,), 0.1),
        'w_ada': nrm(ks[11], (DEPTH, D, 6 * D), 0.5 * D ** -0.5),
        'b_ada': nrm(ks[12], (DEPTH, 6 * D), 0.02),
        'w_attn_in': nrm(ks[13], (N_ATTN_LAYERS, D, ATTN_IN_COLS), D ** -0.5),
        'w_attn_out': nrm(ks[14], (N_ATTN_LAYERS, QKV_W, D), QKV_W ** -0.5),
        'w_conv_in': nrm(ks[15], (N_CONV_LAYERS, D, 3 * D), D ** -0.5),
        'conv_w': nrm(ks[16], (N_CONV_LAYERS, CONV_WIDTH, D), CONV_WIDTH ** -0.5),
        'w_conv_out': nrm(ks[17], (N_CONV_LAYERS, D, D), D ** -0.5),
        'w_ffn_gu': nrm(ks[18], (N_ATTN_LAYERS, D, 2 * FFN_DIM), D ** -0.5),
        'w_ffn_down': nrm(ks[19], (N_ATTN_LAYERS, FFN_DIM, D), FFN_DIM ** -0.5),
        'w_router': nrm(ks[20], (N_CONV_LAYERS, D, N_EXPERTS), D ** -0.5),
        'w_exp_gu': nrm(ks[21], (N_CONV_LAYERS, N_EXPERTS, D, 2 * FFN_DIM), D ** -0.5),
        'w_exp_down': nrm(ks[22], (N_CONV_LAYERS, N_EXPERTS, FFN_DIM, D), FFN_DIM ** -0.5),
    }


def reference(x_prompt, x_sample, cache_k, cache_v, cache_kidx, state_conv, page_table, c_prompt, c_sample,
              norm_g, final_norm_g, w_ada, b_ada, w_attn_in, w_attn_out, w_conv_in, conv_w, w_conv_out,
              w_ffn_gu, w_ffn_down, w_router, w_exp_gu, w_exp_down):
    def attn_prompt(j, h):
        return dsa_prompt(h, w_attn_in[j], w_attn_out[j])

    def conv_prompt(j, h):
        zeros = jnp.zeros((h.shape[0], CONV_WIDTH - 1, D_MODEL), h.dtype)
        return short_conv(h, zeros, w_conv_in[j], conv_w[j], w_conv_out[j])

    def attn_sample(j, h):
        return dsa_sample(h, cache_k[j], cache_v[j], cache_kidx[j], page_table, w_attn_in[j], w_attn_out[j])

    def conv_sample(j, h):
        return short_conv(h, state_conv[j], w_conv_in[j], conv_w[j], w_conv_out[j])

    y_prompt, k_p, v_p, ki_p, conv_p = trunk(x_prompt, c_prompt, attn_prompt, conv_prompt, norm_g, final_norm_g,
                                             w_ada, b_ada, w_ffn_gu, w_ffn_down, w_router, w_exp_gu, w_exp_down)
    y_sample, k_s, v_s, ki_s, conv_s = trunk(x_sample, c_sample, attn_sample, conv_sample, norm_g, final_norm_g,
                                             w_ada, b_ada, w_ffn_gu, w_ffn_down, w_router, w_exp_gu, w_exp_down)
    return (y_prompt, y_sample, k_p, v_p, ki_p, conv_p, k_s, v_s, ki_s, conv_s)
```

```python
import functools
import math

import jax
import jax.numpy as jnp
from jax import lax
from jax.experimental import pallas as pl
from jax.experimental.pallas import tpu as pltpu

HEAD_DIM = 128
IDX_DIM = 128
TOPK_MAX = 256
MOE_TOP_K = 2
ROPE_THETA = 10000.0
RMS_EPS = 1e-6
NEG_INF = -1e30
LANES = 128
SUBLANES = 8
VMEM_LIMIT_BYTES = 56 * 1024 * 1024
INT_MIN = -(2 ** 31)

BF16 = jnp.bfloat16
F32 = jnp.float32


def _params(*sem):
    return pltpu.CompilerParams(dimension_semantics=sem, vmem_limit_bytes=VMEM_LIMIT_BYTES)


def _tile(n, pref):
    if n <= pref:
        return n
    t = pref
    while n % t:
        t //= 2
    return t


def _dot(a, b):
    return jnp.dot(a, b, preferred_element_type=F32)


def _dot_nt(a, b):
    return lax.dot_general(a, b, (((1,), (1,)), ((), ())), preferred_element_type=F32)


def _ada_kernel(c_ref, w_ref, b_ref, o_ref):
    c = c_ref[...]
    a = (c * jax.nn.sigmoid(c)).astype(BF16)
    o_ref[...] = _dot(a, w_ref[...].astype(BF16)) + b_ref[...]


def _ada(c_all, w_ada, b_ada):
    depth, d, n = w_ada.shape
    rows = c_all.shape[0]
    tn = _tile(n, 1024)
    return pl.pallas_call(
        _ada_kernel,
        grid=(depth, n // tn),
        in_specs=[
            pl.BlockSpec((rows, d), lambda l, j: (0, 0)),
            pl.BlockSpec((None, d, tn), lambda l, j: (l, 0, j)),
            pl.BlockSpec((None, 1, tn), lambda l, j: (l, 0, j)),
        ],
        out_specs=pl.BlockSpec((None, rows, tn), lambda l, j: (l, 0, j)),
        out_shape=jax.ShapeDtypeStruct((depth, rows, n), F32),
        compiler_params=_params("arbitrary", "arbitrary"),
        name="adaln",
    )(c_all, w_ada, b_ada.reshape(depth, 1, n))


class _Mods:
    def __init__(self, mods, d_model, n_sample):
        self.mods = mods
        self.d = d_model
        self.n_sample = n_sample

    def spec(self, layer, chunk, per_row, tm, tn, grid_rank):
        ncol = self.d // tn
        if per_row:
            assert tm == self.n_sample
            rows, row_blk = tm, 0
        else:
            rows, row_blk = SUBLANES, self.n_sample // SUBLANES
        if grid_rank == 1:
            assert tn == self.d
            return pl.BlockSpec((None, rows, tn), lambda i: (layer, row_blk, chunk))
        return pl.BlockSpec((None, rows, tn), lambda i, j: (layer, row_blk, chunk * ncol + j))


def _mod_rows(ref, per_row):
    v = ref[...]
    return v if per_row else v[0:1, :]


def _rms_mod(x, g, shift, scale):
    y = x * lax.rsqrt(jnp.mean(x * x, axis=-1, keepdims=True) + RMS_EPS)
    return (y * g) * (1.0 + scale) + shift


def _norm_mod_kernel(x_ref, g_ref, sh_ref, sc_ref, h_ref, *, per_row):
    h = _rms_mod(x_ref[...], g_ref[...], _mod_rows(sh_ref, per_row), _mod_rows(sc_ref, per_row))
    h_ref[...] = h.astype(h_ref.dtype)


def _split_bf16(a):
    hi = a.astype(BF16)
    lo = (a - hi.astype(F32)).astype(BF16)
    return hi, lo


def _norm_mod_route_kernel(x_ref, g_ref, sh_ref, sc_ref, wr_ref, h_ref, rt_ref, *, per_row, n_experts):
    h = _rms_mod(x_ref[...], g_ref[...], _mod_rows(sh_ref, per_row), _mod_rows(sc_ref, per_row))
    h_ref[...] = h.astype(h_ref.dtype)
    h_hi, h_lo = _split_bf16(h)
    w_hi, w_lo = _split_bf16(wr_ref[...])
    logits = _dot(h_hi, w_hi) + (_dot(h_hi, w_lo) + _dot(h_lo, w_hi))
    lane = lax.broadcasted_iota(jnp.int32, logits.shape, 1)
    lane_f = lane.astype(F32)
    logits = jnp.where(lane < n_experts, logits, -jnp.inf)
    m1 = jnp.max(logits, axis=-1, keepdims=True)
    i1 = jnp.min(jnp.where(logits == m1, lane_f, float(LANES)), axis=-1, keepdims=True)
    rest = jnp.where(lane_f == i1, -jnp.inf, logits)
    m2 = jnp.max(rest, axis=-1, keepdims=True)
    i2 = jnp.min(jnp.where(rest == m2, lane_f, float(LANES)), axis=-1, keepdims=True)
    e2 = jnp.exp(m2 - m1)
    g1 = 1.0 / (1.0 + e2)
    g2 = e2 / (1.0 + e2)
    rt = jnp.where(lane == 0, i1, 0.0)
    rt = jnp.where(lane == 1, i2, rt)
    rt = jnp.where(lane == 2, g1, rt)
    rt = jnp.where(lane == 3, g2, rt)
    rt_ref[...] = rt


def _norm_mod(x, g_row, mods, layer, shift_chunk, scale_chunk, per_row, w_router=None):
    m, d = x.shape
    tm = _tile(m, 512)
    in_specs = [
        pl.BlockSpec((tm, d), lambda i: (i, 0)),
        pl.BlockSpec((1, d), lambda i: (0, 0)),
        mods.spec(layer, shift_chunk, per_row, tm, d, 1),
        mods.spec(layer, scale_chunk, per_row, tm, d, 1),
    ]
    h_spec = pl.BlockSpec((tm, d), lambda i: (i, 0))
    h_shape = jax.ShapeDtypeStruct((m, d), BF16)
    if w_router is None:
        return pl.pallas_call(
            functools.partial(_norm_mod_kernel, per_row=per_row),
            grid=(m // tm,), in_specs=in_specs, out_specs=h_spec, out_shape=h_shape,
            compiler_params=_params("arbitrary"), name="norm_mod",
        )(x, g_row, mods.mods, mods.mods)
    n_experts = w_router.shape[1]
    wr = jnp.pad(w_router, ((0, 0), (0, LANES - n_experts)))
    return pl.pallas_call(
        functools.partial(_norm_mod_route_kernel, per_row=per_row, n_experts=n_experts),
        grid=(m // tm,),
        in_specs=in_specs + [pl.BlockSpec((d, LANES), lambda i: (0, 0))],
        out_specs=[h_spec, pl.BlockSpec((tm, LANES), lambda i: (i, 0))],
        out_shape=[h_shape, jax.ShapeDtypeStruct((m, LANES), F32)],
        compiler_params=_params("arbitrary"), name="norm_mod_route",
    )(x, g_row, mods.mods, mods.mods, wr)


def _rope_tile(y, cos, sin_signed):
    outs = []
    for hh in range(y.shape[1] // HEAD_DIM):
        a = y[:, hh * HEAD_DIM:(hh + 1) * HEAD_DIM]
        outs.append(a * cos + pltpu.roll(a, HEAD_DIM // 2, axis=1) * sin_signed)
    return outs


def _proj_kernel(*refs, rope, scale, want_flat, want_heads):
    h_ref, w_ref = refs[0], refs[1]
    pos = 2
    if rope:
        cos_ref, sin_ref = refs[2], refs[3]
        pos = 4
    outs = refs[pos:]
    y = _dot(h_ref[...], w_ref[...].astype(BF16))
    n_heads = y.shape[1] // HEAD_DIM
    if rope:
        heads = _rope_tile(y, cos_ref[...], sin_ref[...])
    else:
        heads = [y[:, hh * HEAD_DIM:(hh + 1) * HEAD_DIM] for hh in range(n_heads)]
    k = 0
    if want_flat:
        for hh in range(n_heads):
            outs[k][:, hh * HEAD_DIM:(hh + 1) * HEAD_DIM] = heads[hh]
        k += 1
    if want_heads:
        for hh in range(n_heads):
            outs[k][hh] = (heads[hh] * scale).astype(BF16)


def _proj(h, w, col0, ncols, cos=None, sin=None, *, scale=1.0, want_flat=False, want_heads=False):
    m, kdim = h.shape
    tm = _tile(m, 1024)
    tn = _tile(math.gcd(ncols, col0) if col0 else ncols, 512)
    assert col0 % tn == 0 and ncols % tn == 0 and tn % HEAD_DIM == 0
    rope = cos is not None
    in_specs = [
        pl.BlockSpec((tm, kdim), lambda i, j: (i, 0)),
        pl.BlockSpec((kdim, tn), lambda i, j: (0, col0 // tn + j)),
    ]
    args = [h, w]
    if rope:
        in_specs += [pl.BlockSpec((tm, HEAD_DIM), lambda i, j: (i, 0))] * 2
        args += [cos, sin]
    out_specs, out_shape = [], []
    if want_flat:
        out_specs.append(pl.BlockSpec((tm, tn), lambda i, j: (i, j)))
        out_shape.append(jax.ShapeDtypeStruct((m, ncols), F32))
    if want_heads:
        hpt = tn // HEAD_DIM
        out_specs.append(pl.BlockSpec((hpt, tm, HEAD_DIM), lambda i, j: (j, i, 0)))
        out_shape.append(jax.ShapeDtypeStruct((ncols // HEAD_DIM, m, HEAD_DIM), BF16))
    return pl.pallas_call(
        functools.partial(_proj_kernel, rope=rope, scale=scale, want_flat=want_flat, want_heads=want_heads),
        grid=(m // tm, ncols // tn), in_specs=in_specs, out_specs=out_specs, out_shape=out_shape,
        compiler_params=_params("arbitrary", "arbitrary"), name="proj",
    )(*args)


def _proj_tail_kernel(h_ref, w_ref, cos_ref, sin_ref, ki_ref, kib_ref, wi_ref, *, wi_scale):
    y = _dot(h_ref[...], w_ref[...].astype(BF16))
    ki = _rope_tile(y[:, :IDX_DIM], cos_ref[...], sin_ref[...])[0]
    ki_ref[...] = ki
    kib_ref[...] = ki.astype(BF16)
    wi_ref[...] = y[:, IDX_DIM:] * wi_scale


def _proj_tail(h, w_tail, cos, sin, wi_scale):
    m, kdim = h.shape
    tm = _tile(m, 1024)
    return pl.pallas_call(
        functools.partial(_proj_tail_kernel, wi_scale=wi_scale),
        grid=(m // tm,),
        in_specs=[
            pl.BlockSpec((tm, kdim), lambda i: (i, 0)),
            pl.BlockSpec((kdim, 2 * LANES), lambda i: (0, 0)),
            pl.BlockSpec((tm, HEAD_DIM), lambda i: (i, 0)),
            pl.BlockSpec((tm, HEAD_DIM), lambda i: (i, 0)),
        ],
        out_specs=[pl.BlockSpec((tm, LANES), lambda i: (i, 0))] * 3,
        out_shape=[jax.ShapeDtypeStruct((m, IDX_DIM), F32), jax.ShapeDtypeStruct((m, IDX_DIM), BF16),
                   jax.ShapeDtypeStruct((m, LANES), F32)],
        compiler_params=_params("arbitrary"), name="proj_tail",
    )(h, w_tail, cos, sin)


def _mm_res_kernel(h_ref, w_ref, x_ref, g_ref, o_ref, *, per_row):
    y = _dot(h_ref[...], w_ref[...].astype(BF16))
    o_ref[...] = x_ref[...] + _mod_rows(g_ref, per_row) * y


def _mm_res(h, w, x, mods, layer, gate_chunk, per_row):
    m, kdim = h.shape
    n = w.shape[1]
    tm = _tile(m, 1024)
    tn = _tile(n, 512)
    return pl.pallas_call(
        functools.partial(_mm_res_kernel, per_row=per_row),
        grid=(m // tm, n // tn),
        in_specs=[
            pl.BlockSpec((tm, kdim), lambda i, j: (i, 0)),
            pl.BlockSpec((kdim, tn), lambda i, j: (0, j)),
            pl.BlockSpec((tm, tn), lambda i, j: (i, j)),
            mods.spec(layer, gate_chunk, per_row, tm, tn, 2),
        ],
        out_specs=pl.BlockSpec((tm, tn), lambda i, j: (i, j)),
        out_shape=jax.ShapeDtypeStruct((m, n), F32),
        compiler_params=_params("arbitrary", "arbitrary"), name="mm_res",
    )(h, w, x, mods.mods)


def _sort_key(scores):
    scores = jnp.where(scores == 0.0, 0.0, scores)
    bits = pltpu.bitcast(scores, jnp.int32)
    return jnp.where(bits < 0, bits ^ jnp.int32(0x7FFFFFFF), bits)


def _count_rows(keys_ref, n_chunks, chunk, pred):
    rows = keys_ref.shape[0]

    def body(c, acc):
        base = pl.multiple_of(c * chunk, chunk)
        for j in range(chunk // LANES):
            kt = keys_ref[:, pl.ds(base + j * LANES, LANES)]
            col = base + j * LANES + lax.broadcasted_iota(jnp.int32, (rows, LANES), 1)
            acc = acc + jnp.where(pred(kt, col), 1.0, 0.0)
        return acc

    acc = lax.fori_loop(0, n_chunks, body, jnp.zeros((rows, LANES), F32))
    return jnp.sum(acc, axis=-1, keepdims=True)


def _select_topk(keys_ref, n_chunks, chunk, topk, n_cols_log2):
    rows = keys_ref.shape[0]

    def bit_body(it, carry):
        t, cnt_t = carry
        cand = t + lax.shift_left(jnp.int32(1), 31 - it)
        cnt = _count_rows(keys_ref, n_chunks, chunk, lambda kt, col: kt >= cand)
        take = cnt >= topk
        return jnp.where(take, cand, t), jnp.where(take, cnt, cnt_t)

    t0 = jnp.full((rows, 1), INT_MIN, jnp.int32)
    cnt0 = jnp.full((rows, 1), float(2 ** 30), F32)
    t, cnt_t = lax.fori_loop(0, 32, bit_body, (t0, cnt0))

    def tie_break():
        cnt_gt = _count_rows(keys_ref, n_chunks, chunk, lambda kt, col: kt > t)

        def j_body(it, j):
            cand = j + lax.shift_left(jnp.int32(1), n_cols_log2 - 1 - it)
            cnt = cnt_gt + _count_rows(keys_ref, n_chunks, chunk, lambda kt, col: (kt == t) & (col <= cand))
            return jnp.where(cnt < topk, cand, j)

        j = lax.fori_loop(0, n_cols_log2, j_body, jnp.full((rows, 1), -1, jnp.int32))
        return j + 1

    all_cols = jnp.full((rows, 1), 2 ** n_cols_log2, jnp.int32)
    j = lax.cond(jnp.max(cnt_t) > float(topk), tie_break, lambda: all_cols)
    return t, j


def _dsa_prompt_kernel(qi_ref, wi_ref, kit_ref, q_ref, k_ref, v_ref, o_ref,
                       keys_scr, t_scr, j_scr, m_scr, l_scr, acc_scr,
                       *, tq, sc, topk, n_idx_heads, n_heads, n_cols_log2):
    i = pl.program_id(0)
    c = pl.program_id(1)
    n_need = ((i + 1) * tq + sc - 1) // sc
    qpos = i * tq + lax.broadcasted_iota(jnp.int32, (tq, sc), 0)

    @pl.when(c == 0)
    def _select():
        wi = wi_ref[...]
        wcols = [wi[:, hh:hh + 1] for hh in range(n_idx_heads)]

        def score_chunk(cc, _):
            base = pl.multiple_of(cc * sc, sc)
            kit = kit_ref[:, pl.ds(base, sc)]
            s = jnp.zeros((tq, sc), F32)
            for hh in range(n_idx_heads):
                s = s + wcols[hh] * jnp.maximum(_dot(qi_ref[hh], kit), 0.0)
            kpos = base + lax.broadcasted_iota(jnp.int32, (tq, sc), 1)
            s = jnp.where(kpos <= qpos, s, NEG_INF)
            keys_scr[:, pl.ds(base, sc)] = _sort_key(s)
            return 0

        lax.fori_loop(0, n_need, score_chunk, 0)
        t, j = _select_topk(keys_scr, n_need, sc, topk, n_cols_log2)
        t_scr[...] = t
        j_scr[...] = j
        m_scr[...] = jnp.full(m_scr.shape, -jnp.inf, F32)
        l_scr[...] = jnp.zeros(l_scr.shape, F32)
        acc_scr[...] = jnp.zeros(acc_scr.shape, F32)

    @pl.when(c < n_need)
    def _attend():
        base = pl.multiple_of(c * sc, sc)
        kt = keys_scr[:, pl.ds(base, sc)]
        kpos = base + lax.broadcasted_iota(jnp.int32, (tq, sc), 1)
        t = t_scr[...]
        sel = (kt > t) | ((kt == t) & (kpos <= j_scr[...]))
        bias = jnp.where(sel & (kpos <= qpos), 0.0, NEG_INF)

        def head(hh, _):
            s = _dot_nt(q_ref[hh], k_ref[hh]) + bias
            m_prev = m_scr[hh]
            m_new = jnp.maximum(m_prev, jnp.max(s, axis=-1, keepdims=True))
            alpha = jnp.exp(m_prev - m_new)
            p = jnp.exp(s - m_new)
            l_scr[hh] = alpha * l_scr[hh] + jnp.sum(p, axis=-1, keepdims=True)
            acc_scr[hh] = alpha * acc_scr[hh] + _dot(p.astype(BF16), v_ref[hh])
            m_scr[hh] = m_new
            return 0

        lax.fori_loop(0, n_heads, head, 0)

    @pl.when(c == n_need - 1)
    def _finish():
        for hh in range(n_heads):
            o_ref[:, hh * HEAD_DIM:(hh + 1) * HEAD_DIM] = (acc_scr[hh] / l_scr[hh]).astype(o_ref.dtype)


def _dsa_prompt(qi_h, wi, ki_t, q_h, k_h, v_h, topk):
    n_idx_heads, s, _ = qi_h.shape
    n_heads = q_h.shape[0]
    tq = _tile(s, 256)
    sc = _tile(s, 512)
    n_cols_log2 = max(1, (s - 1).bit_length())

    def kv_map(i, c):
        return (0, jnp.minimum(c, ((i + 1) * tq + sc - 1) // sc - 1), 0)

    return pl.pallas_call(
        functools.partial(_dsa_prompt_kernel, tq=tq, sc=sc, topk=topk, n_idx_heads=n_idx_heads,
                          n_heads=n_heads, n_cols_log2=n_cols_log2),
        grid=(s // tq, s // sc),
        in_specs=[
            pl.BlockSpec((n_idx_heads, tq, IDX_DIM), lambda i, c: (0, i, 0)),
            pl.BlockSpec((tq, LANES), lambda i, c: (i, 0)),
            pl.BlockSpec((IDX_DIM, s), lambda i, c: (0, 0)),
            pl.BlockSpec((n_heads, tq, HEAD_DIM), lambda i, c: (0, i, 0)),
            pl.BlockSpec((n_heads, sc, HEAD_DIM), kv_map),
            pl.BlockSpec((n_heads, sc, HEAD_DIM), kv_map),
        ],
        out_specs=pl.BlockSpec((tq, n_heads * HEAD_DIM), lambda i, c: (i, 0)),
        out_shape=jax.ShapeDtypeStruct((s, n_heads * HEAD_DIM), BF16),
        scratch_shapes=[
            pltpu.VMEM((tq, s), jnp.int32),
            pltpu.VMEM((tq, 1), jnp.int32),
            pltpu.VMEM((tq, 1), jnp.int32),
            pltpu.VMEM((n_heads, tq, 1), F32),
            pltpu.VMEM((n_heads, tq, 1), F32),
            pltpu.VMEM((n_heads, tq, HEAD_DIM), F32),
        ],
        compiler_params=_params("arbitrary", "arbitrary"), name="dsa_prompt",
    )(qi_h, wi, ki_t, q_h, k_h, v_h)


def _idx_scores_kernel(pt_ref, qi_ref, wi_ref, kin_ref, *rest, n_pages, page, idx_scale):
    del pt_ref
    page_refs = rest[:n_pages]
    o_ref, oself_ref = rest[n_pages], rest[n_pages + 1]
    qi = qi_ref[...] * idx_scale
    qib = qi.astype(BF16)
    wi = wi_ref[...]
    for p in range(n_pages):
        d = _dot_nt(qib, page_refs[p][...].astype(BF16))
        o_ref[:, p * page:(p + 1) * page] = jnp.sum(wi * jnp.maximum(d, 0.0), axis=0, keepdims=True)
    kin = kin_ref[...].astype(BF16).astype(F32)
    d_self = jnp.sum(qib.astype(F32) * kin, axis=-1, keepdims=True)
    s_self = jnp.sum(wi * jnp.maximum(d_self, 0.0), axis=0, keepdims=True)
    oself_ref[...] = jnp.broadcast_to(s_self, oself_ref.shape)


def _idx_scores(page_table, qi3, wi_col, ki_new3, cache_ki, idx_scale):
    db, n_pages = page_table.shape
    n_idx_heads = qi3.shape[1]
    page = cache_ki.shape[1]

    def page_spec(p):
        return pl.BlockSpec((None, page, IDX_DIM), lambda b, pt: (pt[b, p], 0, 0))

    grid_spec = pltpu.PrefetchScalarGridSpec(
        num_scalar_prefetch=1, grid=(db,),
        in_specs=[
            pl.BlockSpec((None, n_idx_heads, IDX_DIM), lambda b, pt: (b, 0, 0)),
            pl.BlockSpec((None, n_idx_heads, 1), lambda b, pt: (b, 0, 0)),
            pl.BlockSpec((None, 1, IDX_DIM), lambda b, pt: (b, 0, 0)),
        ] + [page_spec(p) for p in range(n_pages)],
        out_specs=[pl.BlockSpec((None, 1, n_pages * page), lambda b, pt: (b, 0, 0)),
                   pl.BlockSpec((None, 1, LANES), lambda b, pt: (b, 0, 0))],
    )
    return pl.pallas_call(
        functools.partial(_idx_scores_kernel, n_pages=n_pages, page=page, idx_scale=idx_scale),
        grid_spec=grid_spec,
        out_shape=[jax.ShapeDtypeStruct((db, 1, n_pages * page), F32), jax.ShapeDtypeStruct((db, 1, LANES), F32)],
        compiler_params=_params("arbitrary"), name="idx_scores",
    )(page_table, qi3, wi_col, ki_new3, *([cache_ki] * n_pages))


def _sample_select_kernel(sp_ref, sself_ref, bias_ref, keys_scr, *, past, topk, chunk, n_cols_log2):
    rows, width = keys_scr.shape
    keys_scr[:, :past] = _sort_key(sp_ref[...])
    lane = lax.broadcasted_iota(jnp.int32, (rows, LANES), 1)
    keys_scr[:, past:] = jnp.where(lane == 0, _sort_key(sself_ref[...]), INT_MIN)
    t, j = _select_topk(keys_scr, width // chunk, chunk, topk, n_cols_log2)
    for cc in range(width // LANES):
        kt = keys_scr[:, cc * LANES:(cc + 1) * LANES]
        col = cc * LANES + lane
        sel = (kt > t) | ((kt == t) & (col <= j))
        bias_ref[:, cc * LANES:(cc + 1) * LANES] = jnp.where(sel & (col <= past), 0.0, NEG_INF)


def _sample_select(scores_past, score_self, topk):
    db, past = scores_past.shape
    width = past + LANES
    chunk = LANES
    n_cols_log2 = max(1, (width - 1).bit_length())
    return pl.pallas_call(
        functools.partial(_sample_select_kernel, past=past, topk=topk, chunk=chunk, n_cols_log2=n_cols_log2),
        out_shape=jax.ShapeDtypeStruct((db, width), F32),
        scratch_shapes=[pltpu.VMEM((db, width), jnp.int32)],
        compiler_params=pltpu.CompilerParams(vmem_limit_bytes=VMEM_LIMIT_BYTES), name="sample_select",
    )(scores_past, score_self)


def _dsa_sample_kernel(pt_ref, q_ref, kn_ref, vn_ref, bias_ref, bself_ref, k_ref, v_ref, o_ref,
                       qbd_scr, m_scr, l_scr, acc_scr, *, n_heads, scale):
    del pt_ref
    p = pl.program_id(1)
    n_p = pl.num_programs(1)
    width = n_heads * HEAD_DIM
    diag = (lax.broadcasted_iota(jnp.int32, (n_heads, width), 1) // HEAD_DIM
            == lax.broadcasted_iota(jnp.int32, (n_heads, width), 0))

    @pl.when(p == 0)
    def _init():
        qbd_scr[...] = jnp.where(diag, q_ref[...] * scale, 0.0).astype(BF16)
        m_scr[...] = jnp.full(m_scr.shape, -jnp.inf, F32)
        l_scr[...] = jnp.zeros(l_scr.shape, F32)
        acc_scr[...] = jnp.zeros(acc_scr.shape, F32)

    s = _dot_nt(qbd_scr[...], k_ref[...].astype(BF16)) + bias_ref[...]
    m_prev = m_scr[...]
    m_new = jnp.maximum(m_prev, jnp.max(s, axis=-1, keepdims=True))
    alpha = jnp.exp(m_prev - m_new)
    pr = jnp.exp(s - m_new)
    l_scr[...] = alpha * l_scr[...] + jnp.sum(pr, axis=-1, keepdims=True)
    acc_scr[...] = alpha * acc_scr[...] + _dot(pr.astype(BF16), v_ref[...].astype(BF16))
    m_scr[...] = m_new

    @pl.when(p == n_p - 1)
    def _finish():
        kn = kn_ref[...].astype(BF16).astype(F32)
        vn = vn_ref[...].astype(BF16).astype(F32)
        s_self = jnp.sum(qbd_scr[...].astype(F32) * kn, axis=-1, keepdims=True) + bself_ref[...][:, 0:1]
        m_prev = m_scr[...]
        m_new = jnp.maximum(m_prev, s_self)
        alpha = jnp.exp(m_prev - m_new)
        p_self = jnp.exp(s_self - m_new)
        l = alpha * l_scr[...] + p_self
        acc = alpha * acc_scr[...] + p_self.astype(BF16).astype(F32) * vn
        out = jnp.sum(jnp.where(diag, acc / l, 0.0), axis=0, keepdims=True)
        o_ref[...] = out.astype(o_ref.dtype)


def _dsa_sample(page_table, q3, kn3, vn3, bias4, bself3, cache_k2, cache_v2, n_heads, scale):
    db, n_pages = page_table.shape
    page, width = cache_k2.shape[1], cache_k2.shape[2]
    row = lambda b, p, pt: (b, 0, 0)
    grid_spec = pltpu.PrefetchScalarGridSpec(
        num_scalar_prefetch=1, grid=(db, n_pages),
        in_specs=[
            pl.BlockSpec((None, 1, width), row),
            pl.BlockSpec((None, 1, width), row),
            pl.BlockSpec((None, 1, width), row),
            pl.BlockSpec((None, None, 1, page), lambda b, p, pt: (b, p, 0, 0)),
            pl.BlockSpec((None, 1, LANES), row),
            pl.BlockSpec((None, page, width), lambda b, p, pt: (pt[b, p], 0, 0)),
            pl.BlockSpec((None, page, width), lambda b, p, pt: (pt[b, p], 0, 0)),
        ],
        out_specs=pl.BlockSpec((None, 1, width), row),
        scratch_shapes=[
            pltpu.VMEM((n_heads, width), BF16),
            pltpu.VMEM((n_heads, 1), F32),
            pltpu.VMEM((n_heads, 1), F32),
            pltpu.VMEM((n_heads, width), F32),
        ],
    )
    return pl.pallas_call(
        functools.partial(_dsa_sample_kernel, n_heads=n_heads, scale=scale),
        grid_spec=grid_spec,
        out_shape=jax.ShapeDtypeStruct((db, 1, width), BF16),
        compiler_params=_params("arbitrary", "arbitrary"), name="dsa_sample",
    )(page_table, q3, kn3, vn3, bias4, bself3, cache_k2, cache_v2)


def _conv_kernel(*refs, seq_mode, tm):
    if seq_mode:
        h_ref, wb_ref, wc_ref, wu_ref, cw_ref, y_ref, ztail_ref, carry_scr = refs
    else:
        h_ref, wb_ref, wc_ref, wu_ref, cw_ref, s0_ref, s1_ref, y_ref, z_ref = refs
    i = pl.program_id(0)
    j = pl.program_id(1)
    h = h_ref[...]
    b = _dot(h, wb_ref[...].astype(BF16))
    z = _dot(h, wc_ref[...].astype(BF16)) * _dot(h, wu_ref[...].astype(BF16))
    cw = cw_ref[...]
    tn = z.shape[1]
    if seq_mode:
        col = pl.multiple_of(j * tn, tn)

        @pl.when(i == 0)
        def _zero_state():
            carry_scr[:, pl.ds(col, tn)] = jnp.zeros((SUBLANES, tn), F32)

        prev = carry_scr[:, pl.ds(col, tn)]
        p1 = prev[SUBLANES - 1:SUBLANES, :]
        p2 = prev[SUBLANES - 2:SUBLANES - 1, :]
        row = lax.broadcasted_iota(jnp.int32, z.shape, 0)
        zm1 = jnp.where(row == 0, p1, pltpu.roll(z, 1, axis=0))
        zm2 = jnp.where(row == 0, p2, jnp.where(row == 1, p1, pltpu.roll(z, 2, axis=0)))
        tail = z[tm - SUBLANES:, :]
        carry_scr[:, pl.ds(col, tn)] = tail
        ztail_ref[...] = tail
    else:
        zm2 = s0_ref[...]
        zm1 = s1_ref[...]
        z_ref[...] = z
    y = cw[0:1, :] * zm2 + cw[1:2, :] * zm1 + cw[2:3, :] * z
    y_ref[...] = (b * y).astype(y_ref.dtype)


def _conv_mix(h, w_in, conv_w8, state=None):
    m, d = h.shape
    tm = _tile(m, 1024)
    tn = _tile(d, 512)
    nj = d // tn
    seq_mode = state is None
    in_specs = [
        pl.BlockSpec((tm, d), lambda i, j: (i, 0)),
        pl.BlockSpec((d, tn), lambda i, j: (0, j)),
        pl.BlockSpec((d, tn), lambda i, j: (0, nj + j)),
        pl.BlockSpec((d, tn), lambda i, j: (0, 2 * nj + j)),
        pl.BlockSpec((SUBLANES, tn), lambda i, j: (0, j)),
    ]
    args = [h, w_in, w_in, w_in, conv_w8]
    y_spec = pl.BlockSpec((tm, tn), lambda i, j: (i, j))
    y_shape = jax.ShapeDtypeStruct((m, d), BF16)
    if seq_mode:
        out_specs = [y_spec, pl.BlockSpec((None, SUBLANES, tn), lambda i, j: (i, 0, j))]
        out_shape = [y_shape, jax.ShapeDtypeStruct((m // tm, SUBLANES, d), F32)]
        scratch = [pltpu.VMEM((SUBLANES, d), F32)]
    else:
        in_specs += [pl.BlockSpec((tm, tn), lambda i, j: (i, j))] * 2
        args += [state[:, 0, :], state[:, 1, :]]
        out_specs = [y_spec, pl.BlockSpec((tm, tn), lambda i, j: (i, j))]
        out_shape = [y_shape, jax.ShapeDtypeStruct((m, d), F32)]
        scratch = []
    return pl.pallas_call(
        functools.partial(_conv_kernel, seq_mode=seq_mode, tm=tm),
        grid=(m // tm, nj), in_specs=in_specs, out_specs=out_specs, out_shape=out_shape,
        scratch_shapes=scratch,
        compiler_params=_params("arbitrary", "arbitrary"), name="conv_mix",
    )(*args)


def _ffn_kernel(ue_ref, un_ref, ul_ref, ux_ref, x_ref, wg_ref, wu_ref, wd_ref, o_ref, *, rb, sb):
    del ue_ref, ul_ref, ux_ref
    u = pl.program_id(0)
    f = pl.program_id(1)
    nrows = un_ref[u]

    @pl.when(f == 0)
    def _zero():
        o_ref[...] = jnp.zeros(o_ref.shape, F32)

    @pl.when(nrows > 0)
    def _compute():
        wg = wg_ref[...].astype(BF16)
        wu = wu_ref[...].astype(BF16)
        wd = wd_ref[...].astype(BF16)
        for s in range(rb // sb):
            @pl.when(s * sb < nrows)
            def _sub():
                xs = x_ref[s * sb:(s + 1) * sb, :]
                g = _dot(xs, wg)
                hh = (g * jax.nn.sigmoid(g) * _dot(xs, wu)).astype(BF16)
                o_ref[s * sb:(s + 1) * sb, :] += _dot(hh, wd)


def _ffn(x, w_gu, w_down, unit_expert, unit_rows, rb):
    rows, d = x.shape
    n_exp, f_dim = w_down.shape[0], w_down.shape[1]
    tf = _tile(f_dim, 256)
    nf = f_dim // tf
    sb = _tile(rb, 256)
    n_units = rows // rb
    unit_live = (unit_rows > 0).astype(jnp.int32)
    units = jnp.arange(n_units, dtype=jnp.int32)
    last_live = jnp.max(jnp.where(unit_rows > 0, units, 0))
    unit_xblk = jnp.where(unit_rows > 0, units, last_live).astype(jnp.int32)

    def fblk(f, ul, u):
        return f * ul[u] + (nf - 1) * (1 - ul[u])

    grid_spec = pltpu.PrefetchScalarGridSpec(
        num_scalar_prefetch=4, grid=(n_units, nf),
        in_specs=[
            pl.BlockSpec((rb, d), lambda u, f, ue, un, ul, ux: (ux[u], 0)),
            pl.BlockSpec((None, d, tf), lambda u, f, ue, un, ul, ux: (ue[u], 0, fblk(f, ul, u))),
            pl.BlockSpec((None, d, tf), lambda u, f, ue, un, ul, ux: (ue[u], 0, nf + fblk(f, ul, u))),
            pl.BlockSpec((None, tf, d), lambda u, f, ue, un, ul, ux: (ue[u], fblk(f, ul, u), 0)),
        ],
        out_specs=pl.BlockSpec((rb, d), lambda u, f, ue, un, ul, ux: (u, 0)),
    )
    del n_exp
    return pl.pallas_call(
        functools.partial(_ffn_kernel, rb=rb, sb=sb),
        grid_spec=grid_spec,
        out_shape=jax.ShapeDtypeStruct((rows, d), F32),
        compiler_params=_params("arbitrary", "arbitrary"), name="ffn",
    )(unit_expert, unit_rows, unit_live, unit_xblk, x, w_gu, w_gu, w_down)


def _gather_kernel(tok_ref, live_ref, h_ref, z_ref, o_ref, sem, zsem, *, gb):
    g = pl.program_id(0)

    def copy(r):
        return pltpu.make_async_copy(h_ref.at[tok_ref[g * gb + r]], o_ref.at[g * gb + r], sem)

    @pl.when(live_ref[g] > 0)
    def _gather():
        def start(r, _):
            copy(r).start()
            return 0

        def wait(r, _):
            copy(r).wait()
            return 0

        lax.fori_loop(0, gb, start, 0)
        lax.fori_loop(0, gb, wait, 0)

    @pl.when(live_ref[g] == 0)
    def _zero_fill():
        fill = pltpu.make_async_copy(z_ref, o_ref.at[pl.ds(g * gb, gb)], zsem)
        fill.start()
        fill.wait()


def _gather_rows(h, row_tok, blk_live, gb):
    n, d = h.shape
    rows = row_tok.shape[0]
    grid_spec = pltpu.PrefetchScalarGridSpec(
        num_scalar_prefetch=2, grid=(rows // gb,),
        in_specs=[pl.BlockSpec(memory_space=pl.ANY), pl.BlockSpec(memory_space=pl.ANY)],
        out_specs=pl.BlockSpec(memory_space=pl.ANY),
        scratch_shapes=[pltpu.SemaphoreType.DMA(()), pltpu.SemaphoreType.DMA(())],
    )
    out = pl.pallas_call(
        functools.partial(_gather_kernel, gb=gb),
        grid_spec=grid_spec,
        out_shape=jax.ShapeDtypeStruct((rows, 1, d), h.dtype),
        compiler_params=_params("arbitrary"), name="gather_rows",
    )(row_tok, blk_live, h.reshape(n, 1, d), jnp.zeros((gb, 1, d), h.dtype))
    return out.reshape(rows, d)


def _combine_kernel(pos_ref, y_ref, x_ref, rt_ref, g_ref, fg_ref, o_ref, buf, sem, *, tb, per_row, final_norm):
    i = pl.program_id(0)

    def copy(r, k):
        src = y_ref.at[pos_ref[(i * tb + r) * MOE_TOP_K + k]]
        return pltpu.make_async_copy(src, buf.at[k, pl.ds(r, 1), :], sem)

    def start(r, _):
        for k in range(MOE_TOP_K):
            copy(r, k).start()
        return 0

    def wait(r, _):
        for k in range(MOE_TOP_K):
            copy(r, k).wait()
        return 0

    lax.fori_loop(0, tb, start, 0)
    lax.fori_loop(0, tb, wait, 0)
    rt = rt_ref[...]
    lane = lax.broadcasted_iota(jnp.int32, rt.shape, 1)
    f = jnp.zeros(buf.shape[1:], F32)
    for k in range(MOE_TOP_K):
        gk = jnp.sum(jnp.where(lane == MOE_TOP_K + k, rt, 0.0), axis=-1, keepdims=True)
        f = f + buf[k] * gk
    x = x_ref[...] + _mod_rows(g_ref, per_row) * f
    if final_norm:
        x = x * lax.rsqrt(jnp.mean(x * x, axis=-1, keepdims=True) + RMS_EPS) * fg_ref[...]
    o_ref[...] = x


def _combine(pos_flat, yr, x, rt, mods, layer, gate_chunk, per_row, final_g):
    m, d = x.shape
    tb = _tile(m, 128)
    rows = yr.shape[0]
    grid_spec = pltpu.PrefetchScalarGridSpec(
        num_scalar_prefetch=1, grid=(m // tb,),
        in_specs=[
            pl.BlockSpec(memory_space=pl.ANY),
            pl.BlockSpec((tb, d), lambda i, pos: (i, 0)),
            pl.BlockSpec((tb, LANES), lambda i, pos: (i, 0)),
            pl.BlockSpec((None, tb if per_row else SUBLANES, d),
                         lambda i, pos: (layer, 0 if per_row else mods.n_sample // SUBLANES, gate_chunk)),
            pl.BlockSpec((1, d), lambda i, pos: (0, 0)),
        ],
        out_specs=pl.BlockSpec((tb, d), lambda i, pos: (i, 0)),
        scratch_shapes=[pltpu.VMEM((MOE_TOP_K, tb, d), F32), pltpu.SemaphoreType.DMA(())],
    )
    return pl.pallas_call(
        functools.partial(_combine_kernel, tb=tb, per_row=per_row, final_norm=True),
        grid_spec=grid_spec,
        out_shape=jax.ShapeDtypeStruct((m, d), F32),
        compiler_params=_params("arbitrary"), name="moe_combine",
    )(pos_flat, yr.reshape(rows, 1, d), x, rt, mods.mods, final_g)


def _moe_plan(rt, n_experts, rb):
    n = rt.shape[0]
    a = n * MOE_TOP_K
    exp_flat = rt[:, :MOE_TOP_K].astype(jnp.int32).reshape(a)
    onehot = (exp_flat[:, None] == jnp.arange(n_experts, dtype=jnp.int32)[None, :]).astype(jnp.int32)
    csum = jnp.cumsum(onehot, axis=0)
    counts = csum[-1]
    rank = jnp.sum(csum * onehot, axis=1) - 1
    padded = (counts + rb - 1) // rb * rb
    pad_ends = jnp.cumsum(padded)
    pad_starts = pad_ends - padded
    pos = pad_starts[exp_flat] + rank
    n_units = -(-a // rb) + n_experts
    rows = n_units * rb
    tok_flat = jnp.repeat(jnp.arange(n, dtype=jnp.int32), MOE_TOP_K)
    row_tok = jnp.zeros((rows,), jnp.int32).at[pos].set(tok_flat)
    unit_start = jnp.arange(n_units, dtype=jnp.int32) * rb
    unit_expert = jnp.minimum(jnp.searchsorted(pad_ends, unit_start, side='right'), n_experts - 1).astype(jnp.int32)
    unit_rows = jnp.clip(counts[unit_expert] - (unit_start - pad_starts[unit_expert]), 0, rb).astype(jnp.int32)
    last_live = jnp.max(jnp.where(unit_rows > 0, jnp.arange(n_units), 0))
    unit_expert = jnp.where(unit_rows > 0, unit_expert, unit_expert[last_live])
    return pos.astype(jnp.int32), row_tok, unit_expert, unit_rows


def _rope_tables(pos):
    half = HEAD_DIM // 2
    inv = ROPE_THETA ** (-jnp.arange(half, dtype=F32) / half)
    ang = pos.astype(F32)[:, None] * inv[None, :]
    cos, sin = jnp.cos(ang), jnp.sin(ang)
    return jnp.concatenate([cos, cos], axis=-1), jnp.concatenate([-sin, sin], axis=-1)


def kernel(x_prompt, x_sample, cache_k, cache_v, cache_kidx, state_conv, page_table, c_prompt, c_sample,
           norm_g, final_norm_g, w_ada, b_ada, w_attn_in, w_attn_out, w_conv_in, conv_w, w_conv_out,
           w_ffn_gu, w_ffn_down, w_router, w_exp_gu, w_exp_down):
    _, seq, d = x_prompt.shape
    db = x_sample.shape[0]
    n_heads = cache_k.shape[3]
    qkv_w = n_heads * HEAD_DIM
    n_idx_heads = (w_attn_in.shape[2] - 3 * qkv_w - IDX_DIM) // (IDX_DIM + 1)
    idx_w = n_idx_heads * IDX_DIM
    n_pages = page_table.shape[1]
    page = cache_k.shape[2]
    past = n_pages * page
    n_experts = w_router.shape[2]
    assert x_prompt.shape[0] == 1 and x_sample.shape[1] == 1 and db % SUBLANES == 0
    assert n_idx_heads <= LANES and cache_kidx.shape[3] == IDX_DIM and cache_k.shape[4] == HEAD_DIM

    attn_scale = HEAD_DIM ** -0.5
    idx_scale = IDX_DIM ** -0.5
    wi_scale = n_idx_heads ** -0.5

    xp = x_prompt.reshape(seq, d)
    xs = x_sample.reshape(db, d)

    c_all = jnp.concatenate([c_sample, jnp.broadcast_to(c_prompt, (SUBLANES, d))], axis=0)
    mods = _Mods(_ada(c_all, w_ada, b_ada), d, db)

    cos_p, sin_p = _rope_tables(jnp.arange(seq))
    cos_s, sin_s = _rope_tables(jnp.full((db,), past))

    w_in = w_attn_in[0]
    w_tail = jnp.pad(w_in[:, 3 * qkv_w + idx_w:], ((0, 0), (0, 2 * LANES - IDX_DIM - n_idx_heads)))
    g00 = norm_g[0, 0].reshape(1, d)
    g01 = norm_g[0, 1].reshape(1, d)

    hp = _norm_mod(xp, g00, mods, 0, 0, 1, False)
    (q_h,) = _proj(hp, w_in, 0, qkv_w, cos_p, sin_p, scale=attn_scale, want_heads=True)
    k_p, k_h = _proj(hp, w_in, qkv_w, qkv_w, cos_p, sin_p, want_flat=True, want_heads=True)
    v_p, v_h = _proj(hp, w_in, 2 * qkv_w, qkv_w, want_flat=True, want_heads=True)
    (qi_h,) = _proj(hp, w_in, 3 * qkv_w, idx_w, cos_p, sin_p, scale=idx_scale, want_heads=True)
    ki_p, ki_pb, wi_p = _proj_tail(hp, w_tail, cos_p, sin_p, wi_scale)
    topk_p = min(TOPK_MAX, seq // 4)
    attn_p = _dsa_prompt(qi_h, wi_p, ki_pb.T, q_h, k_h, v_h, topk_p)
    xp = _mm_res(attn_p, w_attn_out[0], xp, mods, 0, 2, False)

    hs = _norm_mod(xs, g00, mods, 0, 0, 1, True)
    (q_s,) = _proj(hs, w_in, 0, qkv_w, cos_s, sin_s, want_flat=True)
    (k_s,) = _proj(hs, w_in, qkv_w, qkv_w, cos_s, sin_s, want_flat=True)
    (v_s,) = _proj(hs, w_in, 2 * qkv_w, qkv_w, want_flat=True)
    (qi_s,) = _proj(hs, w_in, 3 * qkv_w, idx_w, cos_s, sin_s, want_flat=True)
    ki_s, _, wi_s = _proj_tail(hs, w_tail, cos_s, sin_s, wi_scale)
    sc_past, sc_self = _idx_scores(
        page_table, qi_s.reshape(db, n_idx_heads, IDX_DIM), wi_s[:, :n_idx_heads].reshape(db, n_idx_heads, 1),
        ki_s.reshape(db, 1, IDX_DIM), cache_kidx[0], idx_scale)
    topk_s = min(TOPK_MAX, (past + 1) // 4)
    bias = _sample_select(sc_past.reshape(db, past), sc_self.reshape(db, LANES), topk_s)
    attn_s = _dsa_sample(
        page_table, q_s.reshape(db, 1, qkv_w), k_s.reshape(db, 1, qkv_w), v_s.reshape(db, 1, qkv_w),
        bias[:, :past].reshape(db, n_pages, 1, page), bias[:, past:].reshape(db, 1, LANES),
        cache_k[0].reshape(-1, page, qkv_w), cache_v[0].reshape(-1, page, qkv_w), n_heads, attn_scale)
    xs = _mm_res(attn_s.reshape(db, qkv_w), w_attn_out[0], xs, mods, 0, 2, True)

    def dense_ffn(x, per_row):
        m = x.shape[0]
        rb = _tile(m, 1024)
        n_units = m // rb
        h = _norm_mod(x, g01, mods, 0, 3, 4, per_row)
        f = _ffn(h, w_ffn_gu, w_ffn_down, jnp.zeros((n_units,), jnp.int32), jnp.full((n_units,), rb, jnp.int32), rb)
        return f

    xp = _gated_add(xp, dense_ffn(xp, False), mods, 0, 5, False)
    xs = _gated_add(xs, dense_ffn(xs, True), mods, 0, 5, True)

    g10 = norm_g[1, 0].reshape(1, d)
    g11 = norm_g[1, 1].reshape(1, d)
    cw8 = jnp.pad(conv_w[0], ((0, SUBLANES - conv_w.shape[1]), (0, 0)))

    hp = _norm_mod(xp, g10, mods, 1, 0, 1, False)
    y_p, ztail = _conv_mix(hp, w_conv_in[0], cw8)
    conv_p = ztail[-1, SUBLANES - 2:, :].reshape(1, 1, 2, d)
    xp = _mm_res(y_p, w_conv_out[0], xp, mods, 1, 2, False)

    hs = _norm_mod(xs, g10, mods, 1, 0, 1, True)
    y_s, z_s = _conv_mix(hs, w_conv_in[0], cw8, state_conv[0])
    conv_s = jnp.stack([state_conv[0][:, 1, :], z_s], axis=1)[None]
    xs = _mm_res(y_s, w_conv_out[0], xs, mods, 1, 2, True)

    hp, rt_p = _norm_mod(xp, g11, mods, 1, 3, 4, False, w_router[0])
    hs, rt_s = _norm_mod(xs, g11, mods, 1, 3, 4, True, w_router[0])
    h_all = jnp.concatenate([hp, hs], axis=0)
    rt_all = jnp.concatenate([rt_p, rt_s], axis=0)
    rb = 1024 if h_all.shape[0] * MOE_TOP_K >= 8192 else 128
    pos, row_tok, unit_expert, unit_rows = _moe_plan(rt_all, n_experts, rb)
    gb = 128
    blk_live = jnp.repeat((unit_rows > 0).astype(jnp.int32), rb // gb)
    xr = _gather_rows(h_all, row_tok, blk_live, gb)
    yr = _ffn(xr, w_exp_gu[0], w_exp_down[0], unit_expert, unit_rows, rb)
    fg = final_norm_g.reshape(1, d)
    y_p = _combine(pos[:seq * MOE_TOP_K], yr, xp, rt_p, mods, 1, 5, False, fg)
    y_s = _combine(pos[seq * MOE_TOP_K:], yr, xs, rt_s, mods, 1, 5, True, fg)

    return (
        y_p.reshape(1, seq, d), y_s.reshape(db, 1, d),
        k_p.reshape(1, 1, seq, n_heads, HEAD_DIM), v_p.reshape(1, 1, seq, n_heads, HEAD_DIM),
        ki_p.reshape(1, 1, seq, IDX_DIM), conv_p,
        k_s.reshape(1, db, 1, n_heads, HEAD_DIM), v_s.reshape(1, db, 1, n_heads, HEAD_DIM),
        ki_s.reshape(1, db, 1, IDX_DIM), conv_s,
    )


def _gated_add_kernel(x_ref, f_ref, g_ref, o_ref, *, per_row):
    o_ref[...] = x_ref[...] + _mod_rows(g_ref, per_row) * f_ref[...]


def _gated_add(x, f, mods, layer, gate_chunk, per_row):
    m, d = x.shape
    tm = _tile(m, 512)
    blk = pl.BlockSpec((tm, d), lambda i: (i, 0))
    return pl.pallas_call(
        functools.partial(_gated_add_kernel, per_row=per_row),
        grid=(m // tm,),
        in_specs=[blk, blk, mods.spec(layer, gate_chunk, per_row, tm, d, 1)],
        out_specs=blk,
        out_shape=jax.ShapeDtypeStruct((m, d), F32),
        compiler_params=_params("arbitrary"), name="gated_add",
    )(x, f, mods.mods)
```

```python
import functools
import math

import jax
import jax.numpy as jnp
from jax import lax
from jax.experimental import pallas as pl
from jax.experimental.pallas import tpu as pltpu

HEAD_DIM = 128
IDX_DIM = 128
TOPK_MAX = 256
MOE_TOP_K = 2
ROPE_THETA = 10000.0
RMS_EPS = 1e-6
NEG_INF = -1e30
LANES = 128
SUBLANES = 8
VMEM_LIMIT_BYTES = 56 * 1024 * 1024
INT_MIN = -(2 ** 31)

BF16 = jnp.bfloat16
F32 = jnp.float32


def _params(*sem):
    return pltpu.CompilerParams(dimension_semantics=sem, vmem_limit_bytes=VMEM_LIMIT_BYTES)


def _tile(n, pref):
    if n <= pref:
        return n
    t = pref
    while n % t:
        t //= 2
    return t


def _dot(a, b):
    return jnp.dot(a, b, preferred_element_type=F32)


def _dot_nt(a, b):
    return lax.dot_general(a, b, (((1,), (1,)), ((), ())), preferred_element_type=F32)


def _ada_kernel(c_ref, w_ref, b_ref, o_ref):
    c = c_ref[...]
    a = (c * jax.nn.sigmoid(c)).astype(BF16)
    o_ref[...] = _dot(a, w_ref[...].astype(BF16)) + b_ref[...]


def _ada(c_all, w_ada, b_ada):
    depth, d, n = w_ada.shape
    rows = c_all.shape[0]
    tn = _tile(n, 1024)
    return pl.pallas_call(
        _ada_kernel,
        grid=(depth, n // tn),
        in_specs=[
            pl.BlockSpec((rows, d), lambda l, j: (0, 0)),
            pl.BlockSpec((None, d, tn), lambda l, j: (l, 0, j)),
            pl.BlockSpec((None, 1, tn), lambda l, j: (l, 0, j)),
        ],
        out_specs=pl.BlockSpec((None, rows, tn), lambda l, j: (l, 0, j)),
        out_shape=jax.ShapeDtypeStruct((depth, rows, n), F32),
        compiler_params=_params("arbitrary", "arbitrary"),
        name="adaln",
    )(c_all, w_ada, b_ada.reshape(depth, 1, n))


class _Mods:
    def __init__(self, mods, d_model, n_sample):
        self.mods = mods
        self.d = d_model
        self.n_sample = n_sample

    def spec(self, layer, chunk, per_row, tm, tn, grid_rank):
        ncol = self.d // tn
        if per_row:
            assert tm == self.n_sample
            rows, row_blk = tm, 0
        else:
            rows, row_blk = SUBLANES, self.n_sample // SUBLANES
        if grid_rank == 1:
            assert tn == self.d
            return pl.BlockSpec((None, rows, tn), lambda i: (layer, row_blk, chunk))
        return pl.BlockSpec((None, rows, tn), lambda i, j: (layer, row_blk, chunk * ncol + j))


def _mod_rows(ref, per_row):
    v = ref[...]
    return v if per_row else v[0:1, :]


def _rms_mod(x, g, shift, scale):
    y = x * lax.rsqrt(jnp.mean(x * x, axis=-1, keepdims=True) + RMS_EPS)
    return (y * g) * (1.0 + scale) + shift


def _norm_mod_kernel(x_ref, g_ref, sh_ref, sc_ref, h_ref, *, per_row):
    h = _rms_mod(x_ref[...], g_ref[...], _mod_rows(sh_ref, per_row), _mod_rows(sc_ref, per_row))
    h_ref[...] = h.astype(h_ref.dtype)


def _split_bf16(a):
    hi = a.astype(BF16)
    lo = (a - hi.astype(F32)).astype(BF16)
    return hi, lo


def _pack_bf16_pairs(h):
    half = h.shape[1] // 2
    lo = pltpu.bitcast(h[:, :half].astype(BF16).astype(F32), jnp.uint32)
    hi = pltpu.bitcast(h[:, half:].astype(BF16).astype(F32), jnp.uint32)
    return lax.shift_right_logical(lo, jnp.uint32(16)) | (hi & jnp.uint32(0xFFFF0000))


def _unpack_bf16_pairs(xu):
    lo = pltpu.bitcast(lax.shift_left(xu, jnp.uint32(16)), F32).astype(BF16)
    hi = pltpu.bitcast(xu & jnp.uint32(0xFFFF0000), F32).astype(BF16)
    return lo, hi


def _norm_mod_route_kernel(x_ref, g_ref, sh_ref, sc_ref, wr_ref, h_ref, rt_ref, *, per_row, n_experts):
    h = _rms_mod(x_ref[...], g_ref[...], _mod_rows(sh_ref, per_row), _mod_rows(sc_ref, per_row))
    h_ref[...] = _pack_bf16_pairs(h)
    h_hi, h_lo = _split_bf16(h)
    w_hi, w_lo = _split_bf16(wr_ref[...])
    logits = _dot(h_hi, w_hi) + (_dot(h_hi, w_lo) + _dot(h_lo, w_hi))
    lane = lax.broadcasted_iota(jnp.int32, logits.shape, 1)
    lane_f = lane.astype(F32)
    logits = jnp.where(lane < n_experts, logits, -jnp.inf)
    m1 = jnp.max(logits, axis=-1, keepdims=True)
    i1 = jnp.min(jnp.where(logits == m1, lane_f, float(LANES)), axis=-1, keepdims=True)
    rest = jnp.where(lane_f == i1, -jnp.inf, logits)
    m2 = jnp.max(rest, axis=-1, keepdims=True)
    i2 = jnp.min(jnp.where(rest == m2, lane_f, float(LANES)), axis=-1, keepdims=True)
    e2 = jnp.exp(m2 - m1)
    g1 = 1.0 / (1.0 + e2)
    g2 = e2 / (1.0 + e2)
    rt = jnp.where(lane == 0, i1, 0.0)
    rt = jnp.where(lane == 1, i2, rt)
    rt = jnp.where(lane == 2, g1, rt)
    rt = jnp.where(lane == 3, g2, rt)
    rt_ref[...] = rt


def _norm_mod(x, g_row, mods, layer, shift_chunk, scale_chunk, per_row, w_router=None):
    m, d = x.shape
    tm = _tile(m, 512)
    in_specs = [
        pl.BlockSpec((tm, d), lambda i: (i, 0)),
        pl.BlockSpec((1, d), lambda i: (0, 0)),
        mods.spec(layer, shift_chunk, per_row, tm, d, 1),
        mods.spec(layer, scale_chunk, per_row, tm, d, 1),
    ]
    h_spec = pl.BlockSpec((tm, d), lambda i: (i, 0))
    h_shape = jax.ShapeDtypeStruct((m, d), BF16)
    if w_router is None:
        return pl.pallas_call(
            functools.partial(_norm_mod_kernel, per_row=per_row),
            grid=(m // tm,), in_specs=in_specs, out_specs=h_spec, out_shape=h_shape,
            compiler_params=_params("arbitrary"), name="norm_mod",
        )(x, g_row, mods.mods, mods.mods)
    n_experts = w_router.shape[1]
    wr = jnp.pad(w_router, ((0, 0), (0, LANES - n_experts)))
    return pl.pallas_call(
        functools.partial(_norm_mod_route_kernel, per_row=per_row, n_experts=n_experts),
        grid=(m // tm,),
        in_specs=in_specs + [pl.BlockSpec((d, LANES), lambda i: (0, 0))],
        out_specs=[pl.BlockSpec((tm, d // 2), lambda i: (i, 0)), pl.BlockSpec((tm, LANES), lambda i: (i, 0))],
        out_shape=[jax.ShapeDtypeStruct((m, d // 2), jnp.uint32), jax.ShapeDtypeStruct((m, LANES), F32)],
        compiler_params=_params("arbitrary"), name="norm_mod_route",
    )(x, g_row, mods.mods, mods.mods, wr)


def _rope_tile(y, cos, sin_signed):
    outs = []
    for hh in range(y.shape[1] // HEAD_DIM):
        a = y[:, hh * HEAD_DIM:(hh + 1) * HEAD_DIM]
        outs.append(a * cos + pltpu.roll(a, HEAD_DIM // 2, axis=1) * sin_signed)
    return outs


def _proj_kernel(*refs, rope, scale, want_flat, want_heads):
    h_ref, w_ref = refs[0], refs[1]
    pos = 2
    if rope:
        cos_ref, sin_ref = refs[2], refs[3]
        pos = 4
    outs = refs[pos:]
    y = _dot(h_ref[...], w_ref[...].astype(BF16))
    n_heads = y.shape[1] // HEAD_DIM
    if rope:
        heads = _rope_tile(y, cos_ref[...], sin_ref[...])
    else:
        heads = [y[:, hh * HEAD_DIM:(hh + 1) * HEAD_DIM] for hh in range(n_heads)]
    k = 0
    if want_flat:
        for hh in range(n_heads):
            outs[k][:, hh * HEAD_DIM:(hh + 1) * HEAD_DIM] = heads[hh]
        k += 1
    if want_heads == "rows":
        for hh in range(n_heads):
            outs[k][hh] = (heads[hh] * scale).astype(BF16)
    elif want_heads == "cols":
        for hh in range(n_heads):
            outs[k][hh] = (heads[hh] * scale).T.astype(BF16)


def _proj(h, w, col0, ncols, cos=None, sin=None, *, scale=1.0, want_flat=False, want_heads=None):
    m, kdim = h.shape
    tm = _tile(m, 1024)
    tn = _tile(math.gcd(ncols, col0) if col0 else ncols, 512)
    assert col0 % tn == 0 and ncols % tn == 0 and tn % HEAD_DIM == 0
    rope = cos is not None
    in_specs = [
        pl.BlockSpec((tm, kdim), lambda i, j: (i, 0)),
        pl.BlockSpec((kdim, tn), lambda i, j: (0, col0 // tn + j)),
    ]
    args = [h, w]
    if rope:
        in_specs += [pl.BlockSpec((tm, HEAD_DIM), lambda i, j: (i, 0))] * 2
        args += [cos, sin]
    out_specs, out_shape = [], []
    if want_flat:
        out_specs.append(pl.BlockSpec((tm, tn), lambda i, j: (i, j)))
        out_shape.append(jax.ShapeDtypeStruct((m, ncols), F32))
    hpt = tn // HEAD_DIM
    if want_heads == "rows":
        out_specs.append(pl.BlockSpec((hpt, tm, HEAD_DIM), lambda i, j: (j, i, 0)))
        out_shape.append(jax.ShapeDtypeStruct((ncols // HEAD_DIM, m, HEAD_DIM), BF16))
    elif want_heads == "cols":
        out_specs.append(pl.BlockSpec((hpt, HEAD_DIM, tm), lambda i, j: (j, 0, i)))
        out_shape.append(jax.ShapeDtypeStruct((ncols // HEAD_DIM, HEAD_DIM, m), BF16))
    else:
        assert want_heads is None
    return pl.pallas_call(
        functools.partial(_proj_kernel, rope=rope, scale=scale, want_flat=want_flat, want_heads=want_heads),
        grid=(m // tm, ncols // tn), in_specs=in_specs, out_specs=out_specs, out_shape=out_shape,
        compiler_params=_params("arbitrary", "arbitrary"), name="proj",
    )(*args)


def _proj_tail_kernel(h_ref, w_ref, cos_ref, sin_ref, ki_ref, kib_ref, wi_ref, *, wi_scale):
    y = _dot(h_ref[...], w_ref[...].astype(BF16))
    ki = _rope_tile(y[:, :IDX_DIM], cos_ref[...], sin_ref[...])[0]
    ki_ref[...] = ki
    kib_ref[...] = ki.astype(BF16)
    wi_ref[...] = y[:, IDX_DIM:] * wi_scale


def _proj_tail(h, w_tail, cos, sin, wi_scale):
    m, kdim = h.shape
    tm = _tile(m, 1024)
    return pl.pallas_call(
        functools.partial(_proj_tail_kernel, wi_scale=wi_scale),
        grid=(m // tm,),
        in_specs=[
            pl.BlockSpec((tm, kdim), lambda i: (i, 0)),
            pl.BlockSpec((kdim, 2 * LANES), lambda i: (0, 0)),
            pl.BlockSpec((tm, HEAD_DIM), lambda i: (i, 0)),
            pl.BlockSpec((tm, HEAD_DIM), lambda i: (i, 0)),
        ],
        out_specs=[pl.BlockSpec((tm, LANES), lambda i: (i, 0))] * 3,
        out_shape=[jax.ShapeDtypeStruct((m, IDX_DIM), F32), jax.ShapeDtypeStruct((m, IDX_DIM), BF16),
                   jax.ShapeDtypeStruct((m, LANES), F32)],
        compiler_params=_params("arbitrary"), name="proj_tail",
    )(h, w_tail, cos, sin)


def _mm_res_kernel(h_ref, w_ref, x_ref, g_ref, o_ref, *, per_row):
    y = _dot(h_ref[...], w_ref[...].astype(BF16))
    o_ref[...] = x_ref[...] + _mod_rows(g_ref, per_row) * y


def _mm_res(h, w, x, mods, layer, gate_chunk, per_row):
    m, kdim = h.shape
    n = w.shape[1]
    tm = _tile(m, 1024)
    tn = _tile(n, 512)
    return pl.pallas_call(
        functools.partial(_mm_res_kernel, per_row=per_row),
        grid=(m // tm, n // tn),
        in_specs=[
            pl.BlockSpec((tm, kdim), lambda i, j: (i, 0)),
            pl.BlockSpec((kdim, tn), lambda i, j: (0, j)),
            pl.BlockSpec((tm, tn), lambda i, j: (i, j)),
            mods.spec(layer, gate_chunk, per_row, tm, tn, 2),
        ],
        out_specs=pl.BlockSpec((tm, tn), lambda i, j: (i, j)),
        out_shape=jax.ShapeDtypeStruct((m, n), F32),
        compiler_params=_params("arbitrary", "arbitrary"), name="mm_res",
    )(h, w, x, mods.mods)


def _sort_key(scores):
    scores = jnp.where(scores == 0.0, 0.0, scores)
    bits = pltpu.bitcast(scores, jnp.int32)
    return jnp.where(bits < 0, bits ^ jnp.int32(0x7FFFFFFF), bits)


def _count_keys(keys_ref, n_chunks, chunk, pred):
    w = keys_ref.shape[1]

    def body(c, acc):
        base = pl.multiple_of(c * chunk, chunk)
        kt = keys_ref[pl.ds(base, chunk), :]
        row = base + lax.broadcasted_iota(jnp.int32, (chunk, w), 0)
        hit = jnp.where(pred(kt, row), 1.0, 0.0)
        return acc + jnp.sum(hit.reshape(chunk // SUBLANES, SUBLANES, w), axis=0)

    acc = lax.fori_loop(0, n_chunks, body, jnp.zeros((SUBLANES, w), F32))
    return jnp.sum(acc, axis=0, keepdims=True)


def _select_topk(keys_ref, n_chunks, chunk, topk, n_rows_log2):
    w = keys_ref.shape[1]

    def bit_body(it, carry):
        t, cnt_t = carry
        cand = t + lax.shift_left(jnp.int32(1), jnp.asarray(31 - it, jnp.int32))
        cnt = _count_keys(keys_ref, n_chunks, chunk, lambda kt, row: kt >= cand)
        take = cnt >= topk
        return jnp.where(take, cand, t), jnp.where(take, cnt, cnt_t)

    t0 = jnp.full((1, w), INT_MIN, jnp.int32)
    cnt0 = jnp.full((1, w), float(2 ** 30), F32)
    t, cnt_t = lax.fori_loop(0, 32, bit_body, (t0, cnt0))

    def tie_break():
        cnt_gt = _count_keys(keys_ref, n_chunks, chunk, lambda kt, row: kt > t)

        def j_body(it, j):
            cand = j + lax.shift_left(jnp.int32(1), jnp.asarray(n_rows_log2 - 1 - it, jnp.int32))
            cnt = cnt_gt + _count_keys(keys_ref, n_chunks, chunk, lambda kt, row: (kt == t) & (row <= cand))
            return jnp.where(cnt < topk, cand, j)

        j = lax.fori_loop(0, n_rows_log2, j_body, jnp.full((1, w), -1, jnp.int32))
        return j + 1

    all_rows = jnp.full((1, w), 2 ** n_rows_log2, jnp.int32)
    j = lax.cond(jnp.max(cnt_t) > float(topk), tie_break, lambda: all_rows)
    return t, j


def _dsa_prompt_kernel(qi_ref, wit_ref, ki_ref, q_ref, k_ref, vt_ref, o_ref,
                       keys_scr, t_scr, j_scr, m_scr, l_scr, acc_scr,
                       *, tq, sc, topk, n_idx_heads, n_heads, n_rows_log2, hpg):
    i = pl.program_id(0)
    c = pl.program_id(1)
    n_need = ((i + 1) * tq + sc - 1) // sc
    qpos = i * tq + lax.broadcasted_iota(jnp.int32, (sc, tq), 1)

    @pl.when(c == 0)
    def _select():
        wit = wit_ref[...]

        def score_chunk(cc, _):
            base = pl.multiple_of(cc * sc, sc)
            ki = ki_ref[pl.ds(base, sc), :]
            s = jnp.zeros((sc, tq), F32)
            for hh in range(n_idx_heads):
                s = s + wit[hh:hh + 1, :] * jnp.maximum(_dot_nt(ki, qi_ref[hh]), 0.0)
            kpos = base + lax.broadcasted_iota(jnp.int32, (sc, tq), 0)
            s = jnp.where(kpos <= qpos, s, NEG_INF)
            keys_scr[pl.ds(base, sc), :] = _sort_key(s)
            return 0

        lax.fori_loop(0, n_need, score_chunk, 0)
        t, j = _select_topk(keys_scr, n_need, sc, topk, n_rows_log2)
        t_scr[...] = t
        j_scr[...] = j
        m_scr[...] = jnp.full(m_scr.shape, -jnp.inf, F32)
        l_scr[...] = jnp.zeros(l_scr.shape, F32)
        acc_scr[...] = jnp.zeros(acc_scr.shape, F32)

    @pl.when(c < n_need)
    def _attend():
        base = pl.multiple_of(c * sc, sc)
        kt = keys_scr[pl.ds(base, sc), :]
        kpos = base + lax.broadcasted_iota(jnp.int32, (sc, tq), 0)
        t = t_scr[...]
        sel = (kt > t) | ((kt == t) & (kpos <= j_scr[...]))
        bias = jnp.where(sel & (kpos <= qpos), 0.0, NEG_INF)

        def head_update(hh, m_prev, l_prev, acc_prev):
            s = _dot_nt(k_ref[hh], q_ref[hh]) + bias
            m_new = jnp.maximum(m_prev, jnp.max(s, axis=0, keepdims=True))
            alpha = jnp.exp(m_prev - m_new)
            p = jnp.exp(s - m_new)
            l_new = alpha * l_prev + jnp.sum(p, axis=0, keepdims=True)
            return m_new, l_new, alpha * acc_prev + _dot(vt_ref[hh], p.astype(BF16))

        def head_group(gg, _):
            hs = [gg * hpg + e for e in range(hpg)]
            prev = [(m_scr[hh], l_scr[hh], acc_scr[hh]) for hh in hs]
            new = [head_update(hh, *st) for hh, st in zip(hs, prev)]
            for hh, (m_new, l_new, acc_new) in zip(hs, new):
                m_scr[hh] = m_new
                l_scr[hh] = l_new
                acc_scr[hh] = acc_new
            return 0

        lax.fori_loop(0, n_heads // hpg, head_group, 0)

    @pl.when(c == n_need - 1)
    def _finish():
        for hh in range(n_heads):
            out_t = acc_scr[hh] / l_scr[hh]
            o_ref[:, hh * HEAD_DIM:(hh + 1) * HEAD_DIM] = out_t.T.astype(o_ref.dtype)


def _dsa_prompt(qi_h, wi_t, ki_b, q_h, k_h, v_ht, topk):
    n_idx_heads, s, _ = qi_h.shape
    n_heads = q_h.shape[0]
    tq = _tile(s, 256)
    sc = _tile(s, 512)
    n_rows_log2 = max(1, (s - 1).bit_length())

    def chunk_of(i, c):
        return jnp.minimum(c, ((i + 1) * tq + sc - 1) // sc - 1)

    return pl.pallas_call(
        functools.partial(_dsa_prompt_kernel, tq=tq, sc=sc, topk=topk, n_idx_heads=n_idx_heads,
                          n_heads=n_heads, n_rows_log2=n_rows_log2, hpg=4 if n_heads % 4 == 0 else 1),
        grid=(s // tq, s // sc),
        in_specs=[
            pl.BlockSpec((n_idx_heads, tq, IDX_DIM), lambda i, c: (0, i, 0)),
            pl.BlockSpec((n_idx_heads, tq), lambda i, c: (0, i)),
            pl.BlockSpec((s, IDX_DIM), lambda i, c: (0, 0)),
            pl.BlockSpec((n_heads, tq, HEAD_DIM), lambda i, c: (0, i, 0)),
            pl.BlockSpec((n_heads, sc, HEAD_DIM), lambda i, c: (0, chunk_of(i, c), 0)),
            pl.BlockSpec((n_heads, HEAD_DIM, sc), lambda i, c: (0, 0, chunk_of(i, c))),
        ],
        out_specs=pl.BlockSpec((tq, n_heads * HEAD_DIM), lambda i, c: (i, 0)),
        out_shape=jax.ShapeDtypeStruct((s, n_heads * HEAD_DIM), BF16),
        scratch_shapes=[
            pltpu.VMEM((s, tq), jnp.int32),
            pltpu.VMEM((1, tq), jnp.int32),
            pltpu.VMEM((1, tq), jnp.int32),
            pltpu.VMEM((n_heads, 1, tq), F32),
            pltpu.VMEM((n_heads, 1, tq), F32),
            pltpu.VMEM((n_heads, HEAD_DIM, tq), F32),
        ],
        compiler_params=_params("arbitrary", "arbitrary"), name="dsa_prompt",
    )(qi_h, wi_t, ki_b, q_h, k_h, v_ht)


def _idx_scores_kernel(pt_ref, qi_ref, wi_ref, kin_ref, *rest, n_pages, page, idx_scale):
    del pt_ref
    page_refs = rest[:n_pages]
    o_ref, oself_ref = rest[n_pages], rest[n_pages + 1]
    qi = qi_ref[...] * idx_scale
    qib = qi.astype(BF16)
    wi = wi_ref[...]
    for p in range(n_pages):
        d = _dot_nt(qib, page_refs[p][...].astype(BF16))
        o_ref[:, p * page:(p + 1) * page] = jnp.sum(wi * jnp.maximum(d, 0.0), axis=0, keepdims=True)
    kin = kin_ref[...].astype(BF16).astype(F32)
    d_self = jnp.sum(qib.astype(F32) * kin, axis=-1, keepdims=True)
    s_self = jnp.sum(wi * jnp.maximum(d_self, 0.0), axis=0, keepdims=True)
    oself_ref[...] = jnp.broadcast_to(s_self, oself_ref.shape)


def _idx_scores(page_table, qi3, wi_col, ki_new3, cache_ki, idx_scale):
    db, n_pages = page_table.shape
    n_idx_heads = qi3.shape[1]
    page = cache_ki.shape[1]

    def page_spec(p):
        return pl.BlockSpec((None, page, IDX_DIM), lambda b, pt: (pt[b, p], 0, 0))

    grid_spec = pltpu.PrefetchScalarGridSpec(
        num_scalar_prefetch=1, grid=(db,),
        in_specs=[
            pl.BlockSpec((None, n_idx_heads, IDX_DIM), lambda b, pt: (b, 0, 0)),
            pl.BlockSpec((None, n_idx_heads, 1), lambda b, pt: (b, 0, 0)),
            pl.BlockSpec((None, 1, IDX_DIM), lambda b, pt: (b, 0, 0)),
        ] + [page_spec(p) for p in range(n_pages)],
        out_specs=[pl.BlockSpec((None, 1, n_pages * page), lambda b, pt: (b, 0, 0)),
                   pl.BlockSpec((None, 1, LANES), lambda b, pt: (b, 0, 0))],
    )
    return pl.pallas_call(
        functools.partial(_idx_scores_kernel, n_pages=n_pages, page=page, idx_scale=idx_scale),
        grid_spec=grid_spec,
        out_shape=[jax.ShapeDtypeStruct((db, 1, n_pages * page), F32), jax.ShapeDtypeStruct((db, 1, LANES), F32)],
        compiler_params=_params("arbitrary"), name="idx_scores",
    )(page_table, qi3, wi_col, ki_new3, *([cache_ki] * n_pages))


def _sample_select_kernel(st_ref, sself_ref, idx_ref, nv_ref, keys_scr, incl_scr, *, past, topk, n_rows_log2):
    n, w = keys_scr.shape
    keys_scr[:past, :] = _sort_key(st_ref[...])
    tail_row = lax.broadcasted_iota(jnp.int32, (n - past, w), 0)
    keys_scr[past:, :] = jnp.where(tail_row == 0, _sort_key(sself_ref[...]), INT_MIN)
    t, j = _select_topk(keys_scr, n // LANES, LANES, topk, n_rows_log2)

    def selected(kt, row):
        return (kt > t) | ((kt == t) & (row <= j))

    ltri = (lax.broadcasted_iota(jnp.int32, (LANES, LANES), 1)
            <= lax.broadcasted_iota(jnp.int32, (LANES, LANES), 0)).astype(BF16)
    running = jnp.zeros((1, w), F32)
    for cc in range(past // LANES):
        kt = keys_scr[cc * LANES:(cc + 1) * LANES, :]
        row = cc * LANES + lax.broadcasted_iota(jnp.int32, (LANES, w), 0)
        sel = jnp.where(selected(kt, row), 1.0, 0.0)
        incl_scr[cc * LANES:(cc + 1) * LANES, :] = running + _dot(ltri, sel.astype(BF16))
        running = running + jnp.sum(sel, axis=0, keepdims=True)
    self_sel = jnp.where(selected(keys_scr[past:past + 1, :], past), 1.0, 0.0)
    out_row = lax.broadcasted_iota(jnp.int32, nv_ref.shape, 0)
    nv_ref[...] = jnp.where(out_row == 0, running, jnp.where(out_row == 1, self_sel, 0.0))

    def invert(jj, _):
        jf = jnp.asarray(jj, jnp.int32).astype(F32)
        cnt = jnp.zeros((1, w), F32)
        for cc in range(past // LANES):
            below = incl_scr[cc * LANES:(cc + 1) * LANES, :] <= jf
            cnt = cnt + jnp.sum(jnp.where(below, 1.0, 0.0), axis=0, keepdims=True)
        idx_ref[pl.ds(jj, 1), :] = jnp.minimum(cnt, float(past - 1)).astype(jnp.int32)
        return 0

    lax.fori_loop(0, topk, invert, 0)


def _sample_select(scores_t, score_self, topk):
    past, db = scores_t.shape
    n = past + LANES
    n_rows_log2 = max(1, (n - 1).bit_length())
    return pl.pallas_call(
        functools.partial(_sample_select_kernel, past=past, topk=topk, n_rows_log2=n_rows_log2),
        out_shape=[jax.ShapeDtypeStruct((topk, db), jnp.int32), jax.ShapeDtypeStruct((SUBLANES, db), F32)],
        scratch_shapes=[pltpu.VMEM((n, db), jnp.int32), pltpu.VMEM((past, db), F32)],
        compiler_params=pltpu.CompilerParams(vmem_limit_bytes=VMEM_LIMIT_BYTES), name="sample_select",
    )(scores_t, score_self)


def _dsa_sample_kernel(slot_ref, nv_ref, ss_ref, q_ref, kn_ref, vn_ref, ck_ref, cv_ref, o_ref,
                       kbuf, vbuf, ksem, vsem, *, topk, scale):
    b = pl.program_id(0)
    nb = pl.num_programs(0)
    cur = lax.rem(b, 2)

    def k_copy(src_row, half, j):
        return pltpu.make_async_copy(ck_ref.at[src_row], kbuf.at[half, j], ksem.at[half])

    def v_copy(src_row, half, j):
        return pltpu.make_async_copy(cv_ref.at[src_row], vbuf.at[half, j], vsem.at[half])

    def issue(seq, half):
        def body(j, _):
            src_row = slot_ref[seq * topk + j]
            k_copy(src_row, half, j).start()
            v_copy(src_row, half, j).start()
            return 0
        lax.fori_loop(0, topk, body, 0)

    @pl.when(b == 0)
    def _first():
        issue(0, 0)

    @pl.when(b + 1 < nb)
    def _ahead():
        issue(b + 1, 1 - cur)

    def wait(j, _):
        k_copy(0, cur, j).wait()
        v_copy(0, cur, j).wait()
        return 0

    lax.fori_loop(0, topk, wait, 0)

    k = kbuf[cur]
    v = vbuf[cur]
    q = q_ref[...] * scale
    s = jnp.sum(k * q[None], axis=-1, keepdims=True)
    s = jnp.where(lax.broadcasted_iota(jnp.int32, s.shape, 0) < nv_ref[b], s, NEG_INF)
    s_self = jnp.sum(kn_ref[...] * q, axis=-1, keepdims=True)
    s_self = jnp.where(ss_ref[b] > 0, s_self, NEG_INF)
    m = jnp.maximum(jnp.max(s, axis=0), s_self)
    p = jnp.exp(s - m[None])
    p_self = jnp.exp(s_self - m)
    l = jnp.sum(p, axis=0) + p_self
    acc = jnp.sum(p * v, axis=0) + p_self * vn_ref[...]
    o_ref[...] = (acc / l).astype(o_ref.dtype)


def _dsa_sample(slots_flat, n_valid, self_sel, q3, kn3, vn3, cache_k3, cache_v3, topk, scale):
    db, n_heads, _ = q3.shape
    row = lambda b, sl, nv, ss: (b, 0, 0)
    head_tile = pl.BlockSpec((None, n_heads, HEAD_DIM), row)
    grid_spec = pltpu.PrefetchScalarGridSpec(
        num_scalar_prefetch=3, grid=(db,),
        in_specs=[head_tile, head_tile, head_tile,
                  pl.BlockSpec(memory_space=pl.ANY), pl.BlockSpec(memory_space=pl.ANY)],
        out_specs=head_tile,
        scratch_shapes=[
            pltpu.VMEM((2, topk, n_heads, HEAD_DIM), F32),
            pltpu.VMEM((2, topk, n_heads, HEAD_DIM), F32),
            pltpu.SemaphoreType.DMA((2,)),
            pltpu.SemaphoreType.DMA((2,)),
        ],
    )
    return pl.pallas_call(
        functools.partial(_dsa_sample_kernel, topk=topk, scale=scale),
        grid_spec=grid_spec,
        out_shape=jax.ShapeDtypeStruct((db, n_heads, HEAD_DIM), BF16),
        compiler_params=_params("arbitrary"), name="dsa_sample",
    )(slots_flat, n_valid, self_sel, q3, kn3, vn3, cache_k3, cache_v3)


def _conv_kernel(*refs, seq_mode, tm):
    if seq_mode:
        h_ref, wb_ref, wc_ref, wu_ref, cw_ref, y_ref, ztail_ref, carry_scr = refs
    else:
        h_ref, wb_ref, wc_ref, wu_ref, cw_ref, s0_ref, s1_ref, y_ref, z_ref = refs
    i = pl.program_id(0)
    j = pl.program_id(1)
    h = h_ref[...]
    b = _dot(h, wb_ref[...].astype(BF16))
    z = _dot(h, wc_ref[...].astype(BF16)) * _dot(h, wu_ref[...].astype(BF16))
    cw = cw_ref[...]
    tn = z.shape[1]
    if seq_mode:
        col = pl.multiple_of(j * tn, tn)

        @pl.when(i == 0)
        def _zero_state():
            carry_scr[:, pl.ds(col, tn)] = jnp.zeros((SUBLANES, tn), F32)

        prev = carry_scr[:, pl.ds(col, tn)]
        p1 = prev[SUBLANES - 1:SUBLANES, :]
        p2 = prev[SUBLANES - 2:SUBLANES - 1, :]
        row = lax.broadcasted_iota(jnp.int32, z.shape, 0)
        zm1 = jnp.where(row == 0, p1, pltpu.roll(z, 1, axis=0))
        zm2 = jnp.where(row == 0, p2, jnp.where(row == 1, p1, pltpu.roll(z, 2, axis=0)))
        tail = z[tm - SUBLANES:, :]
        carry_scr[:, pl.ds(col, tn)] = tail
        ztail_ref[...] = tail
    else:
        zm2 = s0_ref[...]
        zm1 = s1_ref[...]
        z_ref[...] = z
    y = cw[0:1, :] * zm2 + cw[1:2, :] * zm1 + cw[2:3, :] * z
    y_ref[...] = (b * y).astype(y_ref.dtype)


def _conv_mix(h, w_in, conv_w8, state=None):
    m, d = h.shape
    tm = _tile(m, 1024)
    tn = _tile(d, 512)
    nj = d // tn
    seq_mode = state is None
    in_specs = [
        pl.BlockSpec((tm, d), lambda i, j: (i, 0)),
        pl.BlockSpec((d, tn), lambda i, j: (0, j)),
        pl.BlockSpec((d, tn), lambda i, j: (0, nj + j)),
        pl.BlockSpec((d, tn), lambda i, j: (0, 2 * nj + j)),
        pl.BlockSpec((SUBLANES, tn), lambda i, j: (0, j)),
    ]
    args = [h, w_in, w_in, w_in, conv_w8]
    y_spec = pl.BlockSpec((tm, tn), lambda i, j: (i, j))
    y_shape = jax.ShapeDtypeStruct((m, d), BF16)
    if seq_mode:
        out_specs = [y_spec, pl.BlockSpec((None, SUBLANES, tn), lambda i, j: (i, 0, j))]
        out_shape = [y_shape, jax.ShapeDtypeStruct((m // tm, SUBLANES, d), F32)]
        scratch = [pltpu.VMEM((SUBLANES, d), F32)]
    else:
        in_specs += [pl.BlockSpec((tm, tn), lambda i, j: (i, j))] * 2
        args += [state[:, 0, :], state[:, 1, :]]
        out_specs = [y_spec, pl.BlockSpec((tm, tn), lambda i, j: (i, j))]
        out_shape = [y_shape, jax.ShapeDtypeStruct((m, d), F32)]
        scratch = []
    return pl.pallas_call(
        functools.partial(_conv_kernel, seq_mode=seq_mode, tm=tm),
        grid=(m // tm, nj), in_specs=in_specs, out_specs=out_specs, out_shape=out_shape,
        scratch_shapes=scratch,
        compiler_params=_params("arbitrary", "arbitrary"), name="conv_mix",
    )(*args)


def _ffn_kernel(ue_ref, un_ref, ul_ref, ux_ref, x_ref, wg_ref, wu_ref, wd_ref, o_ref, *scratch, rb, sb, packed):
    del ue_ref, ul_ref, ux_ref
    u = pl.program_id(0)
    f = pl.program_id(1)
    nrows = un_ref[u]

    @pl.when(f == 0)
    def _zero():
        o_ref[...] = jnp.zeros(o_ref.shape, F32)

    if packed:
        xb_ref = scratch[0]

        @pl.when((f == 0) & (nrows > 0))
        def _unpack():
            half = x_ref.shape[1]
            lo, hi = _unpack_bf16_pairs(x_ref[...])
            xb_ref[:, :half] = lo
            xb_ref[:, half:] = hi
    else:
        xb_ref = x_ref

    def sub_block(s, wg, wu, wd):
        xs = xb_ref[s * sb:(s + 1) * sb, :]
        g = _dot(xs, wg)
        hh = (g * jax.nn.sigmoid(g) * _dot(xs, wu)).astype(BF16)
        o_ref[s * sb:(s + 1) * sb, :] += _dot(hh, wd)

    def compute():
        wg = wg_ref[...].astype(BF16)
        wu = wu_ref[...].astype(BF16)
        wd = wd_ref[...].astype(BF16)
        for s in range(rb // sb):
            if packed and s > 0:
                pl.when(s * sb < nrows)(functools.partial(sub_block, s, wg, wu, wd))
            else:
                sub_block(s, wg, wu, wd)

    if packed:
        pl.when(nrows > 0)(compute)
    else:
        compute()


def _ffn(x, w_gu, w_down, unit_expert, unit_rows, rb, packed):
    rows = x.shape[0]
    d = w_down.shape[2]
    n_exp, f_dim = w_down.shape[0], w_down.shape[1]
    tf = _tile(f_dim, 256)
    nf = f_dim // tf
    sb = _tile(rb, 256)
    n_units = rows // rb
    unit_live = (unit_rows > 0).astype(jnp.int32)
    units = jnp.arange(n_units, dtype=jnp.int32)
    last_live = jnp.max(jnp.where(unit_rows > 0, units, 0))
    unit_xblk = jnp.where(unit_rows > 0, units, last_live).astype(jnp.int32)

    def fblk(f, ul, u):
        return f * ul[u] + (nf - 1) * (1 - ul[u])

    grid_spec = pltpu.PrefetchScalarGridSpec(
        num_scalar_prefetch=4, grid=(n_units, nf),
        in_specs=[
            pl.BlockSpec((rb, x.shape[1]), lambda u, f, ue, un, ul, ux: (ux[u], 0)),
            pl.BlockSpec((None, d, tf), lambda u, f, ue, un, ul, ux: (ue[u], 0, fblk(f, ul, u))),
            pl.BlockSpec((None, d, tf), lambda u, f, ue, un, ul, ux: (ue[u], 0, nf + fblk(f, ul, u))),
            pl.BlockSpec((None, tf, d), lambda u, f, ue, un, ul, ux: (ue[u], fblk(f, ul, u), 0)),
        ],
        out_specs=pl.BlockSpec((rb, d), lambda u, f, ue, un, ul, ux: (u, 0)),
        scratch_shapes=[pltpu.VMEM((rb, d), BF16)] if packed else [],
    )
    del n_exp
    return pl.pallas_call(
        functools.partial(_ffn_kernel, rb=rb, sb=sb, packed=packed),
        grid_spec=grid_spec,
        out_shape=jax.ShapeDtypeStruct((rows, d), F32),
        compiler_params=_params("arbitrary", "arbitrary"), name="ffn",
    )(unit_expert, unit_rows, unit_live, unit_xblk, x, w_gu, w_gu, w_down)


def _gather_kernel(tok_ref, live_ref, h_ref, z_ref, o_ref, sem, zsem, *, gb):
    g = pl.program_id(0)

    def copy(r):
        row = g * gb + r
        return pltpu.make_async_copy(h_ref.at[pl.ds(tok_ref[row], 1), :], o_ref.at[pl.ds(row, 1), :], sem)

    @pl.when(live_ref[g] > 0)
    def _gather():
        def start(r, _):
            copy(r).start()
            return 0

        def wait(r, _):
            copy(r).wait()
            return 0

        lax.fori_loop(0, gb, start, 0)
        lax.fori_loop(0, gb, wait, 0)

    @pl.when(live_ref[g] == 0)
    def _zero_fill():
        fill = pltpu.make_async_copy(z_ref, o_ref.at[pl.ds(g * gb, gb), :], zsem)
        fill.start()
        fill.wait()


def _gather_rows(h, row_tok, blk_live, gb):
    n, d = h.shape
    rows = row_tok.shape[0]
    assert h.dtype.itemsize == 4
    grid_spec = pltpu.PrefetchScalarGridSpec(
        num_scalar_prefetch=2, grid=(rows // gb,),
        in_specs=[pl.BlockSpec(memory_space=pl.ANY), pl.BlockSpec(memory_space=pl.ANY)],
        out_specs=pl.BlockSpec(memory_space=pl.ANY),
        scratch_shapes=[pltpu.SemaphoreType.DMA(()), pltpu.SemaphoreType.DMA(())],
    )
    out = pl.pallas_call(
        functools.partial(_gather_kernel, gb=gb),
        grid_spec=grid_spec,
        out_shape=jax.ShapeDtypeStruct((rows, d), h.dtype),
        compiler_params=_params("arbitrary"), name="gather_rows",
    )(row_tok, blk_live, h, jnp.zeros((gb, d), h.dtype))
    return out


def _combine_kernel(pos_ref, y_ref, x_ref, rt_ref, g_ref, fg_ref, o_ref, buf, sem, *, tb, per_row, final_norm):
    i = pl.program_id(0)

    def copy(r, k):
        src = y_ref.at[pl.ds(pos_ref[(i * tb + r) * MOE_TOP_K + k], 1), :]
        return pltpu.make_async_copy(src, buf.at[k, pl.ds(r, 1), :], sem)

    def start(r, _):
        for k in range(MOE_TOP_K):
            copy(r, k).start()
        return 0

    def wait(r, _):
        for k in range(MOE_TOP_K):
            copy(r, k).wait()
        return 0

    lax.fori_loop(0, tb, start, 0)
    lax.fori_loop(0, tb, wait, 0)
    rt = rt_ref[...]
    lane = lax.broadcasted_iota(jnp.int32, rt.shape, 1)
    f = jnp.zeros(buf.shape[1:], F32)
    for k in range(MOE_TOP_K):
        gk = jnp.sum(jnp.where(lane == MOE_TOP_K + k, rt, 0.0), axis=-1, keepdims=True)
        f = f + buf[k] * gk
    x = x_ref[...] + _mod_rows(g_ref, per_row) * f
    if final_norm:
        x = x * lax.rsqrt(jnp.mean(x * x, axis=-1, keepdims=True) + RMS_EPS) * fg_ref[...]
    o_ref[...] = x


def _combine(pos_flat, yr, x, rt, mods, layer, gate_chunk, per_row, final_g):
    m, d = x.shape
    tb = _tile(m, 128)
    grid_spec = pltpu.PrefetchScalarGridSpec(
        num_scalar_prefetch=1, grid=(m // tb,),
        in_specs=[
            pl.BlockSpec(memory_space=pl.ANY),
            pl.BlockSpec((tb, d), lambda i, pos: (i, 0)),
            pl.BlockSpec((tb, LANES), lambda i, pos: (i, 0)),
            pl.BlockSpec((None, tb if per_row else SUBLANES, d),
                         lambda i, pos: (layer, 0 if per_row else mods.n_sample // SUBLANES, gate_chunk)),
            pl.BlockSpec((1, d), lambda i, pos: (0, 0)),
        ],
        out_specs=pl.BlockSpec((tb, d), lambda i, pos: (i, 0)),
        scratch_shapes=[pltpu.VMEM((MOE_TOP_K, tb, d), F32), pltpu.SemaphoreType.DMA(())],
    )
    return pl.pallas_call(
        functools.partial(_combine_kernel, tb=tb, per_row=per_row, final_norm=True),
        grid_spec=grid_spec,
        out_shape=jax.ShapeDtypeStruct((m, d), F32),
        compiler_params=_params("arbitrary"), name="moe_combine",
    )(pos_flat, yr, x, rt, mods.mods, final_g)


def _moe_plan(rt, n_experts, rb):
    n = rt.shape[0]
    a = n * MOE_TOP_K
    exp_flat = rt[:, :MOE_TOP_K].astype(jnp.int32).reshape(a)
    onehot = (exp_flat[:, None] == jnp.arange(n_experts, dtype=jnp.int32)[None, :]).astype(jnp.int32)
    csum = jnp.cumsum(onehot, axis=0)
    counts = csum[-1]
    rank = jnp.sum(csum * onehot, axis=1) - 1
    padded = (counts + rb - 1) // rb * rb
    pad_ends = jnp.cumsum(padded)
    pad_starts = pad_ends - padded
    pos = pad_starts[exp_flat] + rank
    n_units = -(-a // rb) + n_experts
    rows = n_units * rb
    tok_flat = jnp.repeat(jnp.arange(n, dtype=jnp.int32), MOE_TOP_K)
    row_tok = jnp.zeros((rows,), jnp.int32).at[pos].set(tok_flat)
    unit_start = jnp.arange(n_units, dtype=jnp.int32) * rb
    unit_expert = jnp.minimum(jnp.searchsorted(pad_ends, unit_start, side='right'), n_experts - 1).astype(jnp.int32)
    unit_rows = jnp.clip(counts[unit_expert] - (unit_start - pad_starts[unit_expert]), 0, rb).astype(jnp.int32)
    last_live = jnp.max(jnp.where(unit_rows > 0, jnp.arange(n_units), 0))
    unit_expert = jnp.where(unit_rows > 0, unit_expert, unit_expert[last_live])
    return pos.astype(jnp.int32), row_tok, unit_expert, unit_rows


def _rope_tables(pos):
    half = HEAD_DIM // 2
    inv = ROPE_THETA ** (-jnp.arange(half, dtype=F32) / half)
    ang = pos.astype(F32)[:, None] * inv[None, :]
    cos, sin = jnp.cos(ang), jnp.sin(ang)
    return jnp.concatenate([cos, cos], axis=-1), jnp.concatenate([-sin, sin], axis=-1)


def kernel(x_prompt, x_sample, cache_k, cache_v, cache_kidx, state_conv, page_table, c_prompt, c_sample,
           norm_g, final_norm_g, w_ada, b_ada, w_attn_in, w_attn_out, w_conv_in, conv_w, w_conv_out,
           w_ffn_gu, w_ffn_down, w_router, w_exp_gu, w_exp_down):
    _, seq, d = x_prompt.shape
    db = x_sample.shape[0]
    n_heads = cache_k.shape[3]
    qkv_w = n_heads * HEAD_DIM
    n_idx_heads = (w_attn_in.shape[2] - 3 * qkv_w - IDX_DIM) // (IDX_DIM + 1)
    idx_w = n_idx_heads * IDX_DIM
    n_pages = page_table.shape[1]
    page = cache_k.shape[2]
    past = n_pages * page
    n_experts = w_router.shape[2]
    assert x_prompt.shape[0] == 1 and x_sample.shape[1] == 1 and db % SUBLANES == 0
    assert n_idx_heads <= LANES and cache_kidx.shape[3] == IDX_DIM and cache_k.shape[4] == HEAD_DIM

    attn_scale = HEAD_DIM ** -0.5
    idx_scale = IDX_DIM ** -0.5
    wi_scale = n_idx_heads ** -0.5

    xp = x_prompt.reshape(seq, d)
    xs = x_sample.reshape(db, d)

    c_all = jnp.concatenate([c_sample, jnp.broadcast_to(c_prompt, (SUBLANES, d))], axis=0)
    mods = _Mods(_ada(c_all, w_ada, b_ada), d, db)

    cos_p, sin_p = _rope_tables(jnp.arange(seq))
    cos_s, sin_s = _rope_tables(jnp.full((db,), past))

    w_in = w_attn_in[0]
    w_tail = jnp.pad(w_in[:, 3 * qkv_w + idx_w:], ((0, 0), (0, 2 * LANES - IDX_DIM - n_idx_heads)))
    g00 = norm_g[0, 0].reshape(1, d)
    g01 = norm_g[0, 1].reshape(1, d)

    hp = _norm_mod(xp, g00, mods, 0, 0, 1, False)
    (q_h,) = _proj(hp, w_in, 0, qkv_w, cos_p, sin_p, scale=attn_scale, want_heads="rows")
    k_p, k_h = _proj(hp, w_in, qkv_w, qkv_w, cos_p, sin_p, want_flat=True, want_heads="rows")
    v_p, v_ht = _proj(hp, w_in, 2 * qkv_w, qkv_w, want_flat=True, want_heads="cols")
    (qi_h,) = _proj(hp, w_in, 3 * qkv_w, idx_w, cos_p, sin_p, scale=idx_scale, want_heads="rows")
    ki_p, ki_pb, wi_p = _proj_tail(hp, w_tail, cos_p, sin_p, wi_scale)
    topk_p = min(TOPK_MAX, seq // 4)
    attn_p = _dsa_prompt(qi_h, wi_p[:, :n_idx_heads].T, ki_pb, q_h, k_h, v_ht, topk_p)
    xp = _mm_res(attn_p, w_attn_out[0], xp, mods, 0, 2, False)

    hs = _norm_mod(xs, g00, mods, 0, 0, 1, True)
    (q_s,) = _proj(hs, w_in, 0, qkv_w, cos_s, sin_s, want_flat=True)
    (k_s,) = _proj(hs, w_in, qkv_w, qkv_w, cos_s, sin_s, want_flat=True)
    (v_s,) = _proj(hs, w_in, 2 * qkv_w, qkv_w, want_flat=True)
    (qi_s,) = _proj(hs, w_in, 3 * qkv_w, idx_w, cos_s, sin_s, want_flat=True)
    ki_s, _, wi_s = _proj_tail(hs, w_tail, cos_s, sin_s, wi_scale)
    sc_past, sc_self = _idx_scores(
        page_table, qi_s.reshape(db, n_idx_heads, IDX_DIM), wi_s[:, :n_idx_heads].reshape(db, n_idx_heads, 1),
        ki_s.reshape(db, 1, IDX_DIM), cache_kidx[0], idx_scale)
    topk_s = min(TOPK_MAX, (past + 1) // 4)
    idx_t, nv = _sample_select(sc_past.reshape(db, past).T, sc_self.reshape(db, LANES)[:, 0][None, :], topk_s)
    idx = idx_t.T
    slots = jnp.take_along_axis(page_table, idx // page, axis=1) * page + idx % page
    heads3 = lambda a: a.reshape(db, n_heads, HEAD_DIM)
    attn_s = _dsa_sample(
        slots.reshape(-1).astype(jnp.int32), nv[0].astype(jnp.int32), nv[1].astype(jnp.int32),
        heads3(q_s), heads3(k_s), heads3(v_s),
        cache_k.reshape(-1, n_heads, HEAD_DIM), cache_v.reshape(-1, n_heads, HEAD_DIM), topk_s, attn_scale)
    xs = _mm_res(attn_s.reshape(db, qkv_w), w_attn_out[0], xs, mods, 0, 2, True)

    def dense_ffn(x, per_row):
        m = x.shape[0]
        rb = _tile(m, 1024)
        n_units = m // rb
        h = _norm_mod(x, g01, mods, 0, 3, 4, per_row)
        f = _ffn(h, w_ffn_gu, w_ffn_down, jnp.zeros((n_units,), jnp.int32), jnp.full((n_units,), rb, jnp.int32), rb,
                 packed=False)
        return f

    xp = _gated_add(xp, dense_ffn(xp, False), mods, 0, 5, False)
    xs = _gated_add(xs, dense_ffn(xs, True), mods, 0, 5, True)

    g10 = norm_g[1, 0].reshape(1, d)
    g11 = norm_g[1, 1].reshape(1, d)
    cw8 = jnp.pad(conv_w[0], ((0, SUBLANES - conv_w.shape[1]), (0, 0)))

    hp = _norm_mod(xp, g10, mods, 1, 0, 1, False)
    y_p, ztail = _conv_mix(hp, w_conv_in[0], cw8)
    conv_p = ztail[-1, SUBLANES - 2:, :].reshape(1, 1, 2, d)
    xp = _mm_res(y_p, w_conv_out[0], xp, mods, 1, 2, False)

    hs = _norm_mod(xs, g10, mods, 1, 0, 1, True)
    y_s, z_s = _conv_mix(hs, w_conv_in[0], cw8, state_conv[0])
    conv_s = jnp.stack([state_conv[0][:, 1, :], z_s], axis=1)[None]
    xs = _mm_res(y_s, w_conv_out[0], xs, mods, 1, 2, True)

    hp, rt_p = _norm_mod(xp, g11, mods, 1, 3, 4, False, w_router[0])
    hs, rt_s = _norm_mod(xs, g11, mods, 1, 3, 4, True, w_router[0])
    h_all = jnp.concatenate([hp, hs], axis=0)
    rt_all = jnp.concatenate([rt_p, rt_s], axis=0)
    rb = 1024 if h_all.shape[0] * MOE_TOP_K >= 8192 else 128
    pos, row_tok, unit_expert, unit_rows = _moe_plan(rt_all, n_experts, rb)
    gb = 128
    blk_live = jnp.repeat((unit_rows > 0).astype(jnp.int32), rb // gb)
    xr = _gather_rows(h_all, row_tok, blk_live, gb)
    yr = _ffn(xr, w_exp_gu[0], w_exp_down[0], unit_expert, unit_rows, rb, packed=True)
    fg = final_norm_g.reshape(1, d)
    y_p = _combine(pos[:seq * MOE_TOP_K], yr, xp, rt_p, mods, 1, 5, False, fg)
    y_s = _combine(pos[seq * MOE_TOP_K:], yr, xs, rt_s, mods, 1, 5, True, fg)

    return (
        y_p.reshape(1, seq, d), y_s.reshape(db, 1, d),
        k_p.reshape(1, 1, seq, n_heads, HEAD_DIM), v_p.reshape(1, 1, seq, n_heads, HEAD_DIM),
        ki_p.reshape(1, 1, seq, IDX_DIM), conv_p,
        k_s.reshape(1, db, 1, n_heads, HEAD_DIM), v_s.reshape(1, db, 1, n_heads, HEAD_DIM),
        ki_s.reshape(1, db, 1, IDX_DIM), conv_s,
    )


def _gated_add_kernel(x_ref, f_ref, g_ref, o_ref, *, per_row):
    o_ref[...] = x_ref[...] + _mod_rows(g_ref, per_row) * f_ref[...]


def _gated_add(x, f, mods, layer, gate_chunk, per_row):
    m, d = x.shape
    tm = _tile(m, 512)
    blk = pl.BlockSpec((tm, d), lambda i: (i, 0))
    return pl.pallas_call(
        functools.partial(_gated_add_kernel, per_row=per_row),
        grid=(m // tm,),
        in_specs=[blk, blk, mods.spec(layer, gate_chunk, per_row, tm, d, 1)],
        out_specs=blk,
        out_shape=jax.ShapeDtypeStruct((m, d), F32),
        compiler_params=_params("arbitrary"), name="gated_add",
    )(x, f, mods.mods)
```
